```python
import jax, jax.numpy as jnp
from jax import lax
import numpy as np

D_MODEL = 2048
BATCH = 2
SEQ = 4096
DEPTH = 2
DEC_BATCH = 8
DEC_SEQ = 4
PAST_LEN = 16384
PAGE_SIZE = 128

N_RET_LAYERS = (DEPTH + 1) // 2
N_MOBA_LAYERS = DEPTH // 2
RET_HEADS = 8
RET_DK = D_MODEL // RET_HEADS
RET_DV = 2 * RET_DK
RET_CHUNK = 128
ROPE_BASE = 10000.0
MOBA_HEADS = 16
MOBA_HD = D_MODEL // MOBA_HEADS
MOBA_BLOCK = 256
MOBA_TOPK = 3
MOBA_QCHUNK = 16
FFN_DIM = 5632
N_EXPERTS = 8
MOE_TOPK = 2
EXPERT_DIM = 7168
RMS_EPS = 1e-6
GN_EPS = 1e-5

kernel_name = 'retnet_moba_hybrid_step'


def rms_norm(x, w):
    xf = x.astype(jnp.float32)
    y = xf * lax.rsqrt(jnp.mean(xf * xf, axis=-1, keepdims=True) + RMS_EPS)
    return (y * w.astype(jnp.float32)).astype(x.dtype)


def rotary(x, pos):
    half = x.shape[-1] // 2
    inv = ROPE_BASE ** (-jnp.arange(half, dtype=jnp.float32) / half)
    ang = pos.astype(jnp.float32)[:, None] * inv[None, :]
    cos = jnp.cos(ang)[None, :, None, :]
    sin = jnp.sin(ang)[None, :, None, :]
    xf = x.astype(jnp.float32)
    x1, x2 = xf[..., :half], xf[..., half:]
    return jnp.concatenate([x1 * cos - x2 * sin, x1 * sin + x2 * cos], axis=-1)


def retention_log_gamma():
    return jnp.log1p(-jnp.exp2(-5.0 - jnp.arange(RET_HEADS, dtype=jnp.float32)))


def retention_chunkwise(q, k, v, state0, chunk):
    b, s, h, dk = q.shape
    dv = v.shape[-1]
    nc = s // chunk
    lg = retention_log_gamma()
    i = jnp.arange(chunk, dtype=jnp.float32)
    diff = i[:, None] - i[None, :]
    decay = jnp.where(diff[None] >= 0,
                      jnp.exp(jnp.maximum(diff, 0.0)[None] * lg[:, None, None]), 0.0)
    q_decay = jnp.exp((i + 1.0)[:, None] * lg[None, :])
    k_decay = jnp.exp((chunk - 1.0 - i)[:, None] * lg[None, :])
    chunk_decay = jnp.exp(chunk * lg)

    def to_chunks(a):
        return jnp.moveaxis(a.reshape(b, nc, chunk, h, a.shape[-1]), 1, 0)

    def step(state, qkv):
        qc, kc, vc = qkv
        scores = jnp.einsum('bihd,bjhd->bhij', qc, kc) * decay[None]
        inner = jnp.einsum('bhij,bjhe->bihe', scores, vc)
        cross = jnp.einsum('bihd,bhde->bihe', qc, state) * q_decay[None, :, :, None]
        state = state * chunk_decay[None, :, None, None] + jnp.einsum(
            'bjhd,bjhe->bhde', kc * k_decay[None, :, :, None], vc)
        return state, inner + cross

    state, ys = lax.scan(step, state0, (to_chunks(q), to_chunks(k), to_chunks(v)))
    y = jnp.moveaxis(ys, 0, 1).reshape(b, s, h, dv)
    return y, state


def retention_mixer(h, pos, state0, w_in, w_out):
    b, s, _ = h.shape
    hk = RET_HEADS * RET_DK
    hv = RET_HEADS * RET_DV
    q, k, v, g = jnp.split(h @ w_in, [hk, 2 * hk, 2 * hk + hv], axis=-1)
    q = rotary(q.reshape(b, s, RET_HEADS, RET_DK), pos)
    k = rotary(k.reshape(b, s, RET_HEADS, RET_DK), pos) * (RET_DK ** -0.5)
    v = v.reshape(b, s, RET_HEADS, RET_DV).astype(jnp.float32)
    chunk = RET_CHUNK if s % RET_CHUNK == 0 else s
    y, state = retention_chunkwise(q, k, v, state0.astype(jnp.float32), chunk)
    mu = jnp.mean(y, axis=-1, keepdims=True)
    var = jnp.mean(jnp.square(y - mu), axis=-1, keepdims=True)
    y = ((y - mu) * lax.rsqrt(var + GN_EPS)).reshape(b, s, hv).astype(h.dtype)
    return (jax.nn.silu(g) * y) @ w_out, state


def moba_blocks(k, v):
    b, l, h, hd = k.shape
    nb = -(-l // MOBA_BLOCK)
    pad = nb * MOBA_BLOCK - l
    kb = jnp.pad(k, ((0, 0), (0, pad), (0, 0), (0, 0))).reshape(b, nb, MOBA_BLOCK, h, hd)
    vb = jnp.pad(v, ((0, 0), (0, pad), (0, 0), (0, 0))).reshape(b, nb, MOBA_BLOCK, h, hd)
    means = jnp.mean(kb, axis=2, dtype=jnp.float32)
    return kb, vb, means


def moba_attend(q, q_pos, kb, vb, means):
    b, t, h, hd = q.shape
    nb = kb.shape[1]
    own = q_pos // MOBA_BLOCK
    qf = q.astype(jnp.float32)
    gate = jnp.einsum('bthd,bnhd->bthn', qf, means)
    past = jnp.arange(nb)[None, :] < own[:, None]
    gate = jnp.where(past[None, :, None, :], gate, -jnp.inf)
    n_sel = min(MOBA_TOPK, nb)
    _, sel = lax.top_k(gate, n_sel)
    own_b = jnp.broadcast_to(own[None, :, None, None], (b, t, h, 1)).astype(sel.dtype)
    idx = jnp.concatenate([sel, own_b], axis=-1)
    slot_ok = jnp.concatenate([sel < own[None, :, None, None],
                               jnp.ones((b, t, h, 1), dtype=bool)], axis=-1)
    bi = jnp.arange(b)[:, None, None, None]
    hi = jnp.arange(h)[None, None, :, None]
    kg = kb[bi, idx, :, hi, :]
    vg = vb[bi, idx, :, hi, :]
    key_pos = idx[..., None] * MOBA_BLOCK + jnp.arange(MOBA_BLOCK)
    mask = slot_ok[..., None] & (key_pos <= q_pos[None, :, None, None, None])
    s = jnp.einsum('bthd,bthjsd->bthjs', q, kg, preferred_element_type=jnp.float32) * (hd ** -0.5)
    s = jnp.where(mask, s, -jnp.inf)
    p = jax.nn.softmax(s.reshape(b, t, h, -1), axis=-1).reshape(s.shape)
    out = jnp.einsum('bthjs,bthjsd->bthd', p.astype(vg.dtype), vg, preferred_element_type=jnp.float32)
    return out.astype(q.dtype)


def moba_project(h, w_qkv):
    b, s, _ = h.shape
    q, k, v = jnp.split(h @ w_qkv, 3, axis=-1)
    shp = (b, s, MOBA_HEADS, MOBA_HD)
    return q.reshape(shp), k.reshape(shp), v.reshape(shp)


def moba_prompt(h, w_qkv, w_out):
    b, s, d = h.shape
    q, k, v = moba_project(h, w_qkv)
    kb, vb, means = moba_blocks(k, v)
    nq = s // MOBA_QCHUNK
    qc = jnp.moveaxis(q.reshape(b, nq, MOBA_QCHUNK, MOBA_HEADS, MOBA_HD), 1, 0)
    pc = jnp.arange(s, dtype=jnp.int32).reshape(nq, MOBA_QCHUNK)
    o = lax.map(lambda a: moba_attend(a[0], a[1], kb, vb, means), (qc, pc))
    o = jnp.moveaxis(o, 0, 1).reshape(b, s, d)
    return o @ w_out, k, v


def moba_sample(h, cache_k_l, cache_v_l, page_table, w_qkv, w_out):
    b, t, d = h.shape
    past_len = page_table.shape[1] * cache_k_l.shape[1]
    q, k, v = moba_project(h, w_qkv)
    past_k = cache_k_l[page_table].reshape(b, past_len, MOBA_HEADS, MOBA_HD)
    past_v = cache_v_l[page_table].reshape(b, past_len, MOBA_HEADS, MOBA_HD)
    k_all = jnp.concatenate([past_k, k.astype(past_k.dtype)], axis=1)
    v_all = jnp.concatenate([past_v, v.astype(past_v.dtype)], axis=1)
    kb, vb, means = moba_blocks(k_all, v_all)
    q_pos = past_len + jnp.arange(t, dtype=jnp.int32)
    o = moba_attend(q, q_pos, kb, vb, means).reshape(b, t, d)
    return o @ w_out, k, v


def swiglu(h, w_gate, w_up, w_down):
    return (jax.nn.silu(h @ w_gate) * (h @ w_up)) @ w_down


def moe_swiglu(h, w_router, w_gate, w_up, w_down):
    shape = h.shape
    x = h.reshape(-1, shape[-1])
    logits = (x @ w_router).astype(jnp.float32)
    top_v, top_i = lax.top_k(logits, MOE_TOPK)
    wts = jax.nn.softmax(top_v, axis=-1)
    combine = jnp.sum(jax.nn.one_hot(top_i, N_EXPERTS, dtype=jnp.float32) * wts[..., None], axis=1)
    y = jnp.zeros(x.shape, jnp.float32)
    for e in range(N_EXPERTS):
        ye = swiglu(x, w_gate[e], w_up[e], w_down[e]).astype(jnp.float32)
        y = y + combine[:, e:e + 1] * ye
    return y.astype(h.dtype).reshape(shape)


def setup_inputs(seed: int = 0) -> dict:
    key = jax.random.key(seed)
    ks = jax.random.split(key, 20)
    n_pages = PAST_LEN // PAGE_SIZE
    n_used = DEC_BATCH * n_pages
    n_pool = n_used + n_used // 4
    hk = RET_HEADS * RET_DK
    hv = RET_HEADS * RET_DV
    f32 = jnp.float32

    def w(k, shape, fan_in):
        return jax.random.normal(k, shape, f32) * (fan_in ** -0.5)

    page_table = jax.random.permutation(ks[5], n_pool)[:n_used].reshape(DEC_BATCH, n_pages).astype(jnp.int32)
    return {
        'x_prompt': jax.random.normal(ks[0], (BATCH, SEQ, D_MODEL), f32),
        'x_sample': jax.random.normal(ks[1], (DEC_BATCH, DEC_SEQ, D_MODEL), f32),
        'state_ret': 0.5 * jax.random.normal(ks[2], (N_RET_LAYERS, DEC_BATCH, RET_HEADS, RET_DK, RET_DV), f32),
        'cache_k': jax.random.normal(ks[3], (N_MOBA_LAYERS, n_pool, PAGE_SIZE, MOBA_HEADS, MOBA_HD), f32),
        'cache_v': jax.random.normal(ks[4], (N_MOBA_LAYERS, n_pool, PAGE_SIZE, MOBA_HEADS, MOBA_HD), f32),
        'page_table': page_table,
        'norm_w': 1.0 + 0.05 * jax.random.normal(ks[6], (DEPTH, 2, D_MODEL), f32),
        'final_norm_w': 1.0 + 0.05 * jax.random.normal(ks[7], (D_MODEL,), f32),
        'ret_w_in': w(ks[8], (N_RET_LAYERS, D_MODEL, 2 * hk + 2 * hv), D_MODEL),
        'ret_w_out': w(ks[9], (N_RET_LAYERS, hv, D_MODEL), hv),
        'moba_w_qkv': w(ks[10], (N_MOBA_LAYERS, D_MODEL, 3 * D_MODEL), D_MODEL),
        'moba_w_out': w(ks[11], (N_MOBA_LAYERS, D_MODEL, D_MODEL), D_MODEL),
        'ffn_w_gate': w(ks[12], (N_RET_LAYERS, D_MODEL, FFN_DIM), D_MODEL),
        'ffn_w_up': w(ks[13], (N_RET_LAYERS, D_MODEL, FFN_DIM), D_MODEL),
        'ffn_w_down': w(ks[14], (N_RET_LAYERS, FFN_DIM, D_MODEL), FFN_DIM),
        'moe_w_router': w(ks[15], (N_MOBA_LAYERS, D_MODEL, N_EXPERTS), D_MODEL),
        'moe_w_gate': w(ks[16], (N_MOBA_LAYERS, N_EXPERTS, D_MODEL, EXPERT_DIM), D_MODEL),
        'moe_w_up': w(ks[17], (N_MOBA_LAYERS, N_EXPERTS, D_MODEL, EXPERT_DIM), D_MODEL),
        'moe_w_down': w(ks[18], (N_MOBA_LAYERS, N_EXPERTS, EXPERT_DIM, D_MODEL), EXPERT_DIM),
    }


def reference(x_prompt, x_sample, state_ret, cache_k, cache_v, page_table, norm_w, final_norm_w,
              ret_w_in, ret_w_out, moba_w_qkv, moba_w_out, ffn_w_gate, ffn_w_up, ffn_w_down,
              moe_w_router, moe_w_gate, moe_w_up, moe_w_down):
    xp, xs = x_prompt, x_sample
    bp, sp, _ = xp.shape
    past_len = page_table.shape[1] * cache_k.shape[2]
    pos_p = jnp.arange(sp, dtype=jnp.int32)
    pos_s = past_len + jnp.arange(xs.shape[1], dtype=jnp.int32)
    ret_p, ret_s, kp, vp, ksm, vsm = [], [], [], [], [], []
    for layer in range(DEPTH):
        li = layer // 2
        hp = rms_norm(xp, norm_w[layer, 0])
        hs = rms_norm(xs, norm_w[layer, 0])
        if layer % 2 == 0:
            zero_state = jnp.zeros((bp, RET_HEADS, RET_DK, RET_DV), jnp.float32)
            op, st_p = retention_mixer(hp, pos_p, zero_state, ret_w_in[li], ret_w_out[li])
            os_, st_s = retention_mixer(hs, pos_s, state_ret[li], ret_w_in[li], ret_w_out[li])
            ret_p.append(st_p)
            ret_s.append(st_s)
        else:
            op, k_p, v_p = moba_prompt(hp, moba_w_qkv[li], moba_w_out[li])
            os_, k_s, v_s = moba_sample(hs, cache_k[li], cache_v[li], page_table, moba_w_qkv[li], moba_w_out[li])
            kp.append(k_p)
            vp.append(v_p)
            ksm.append(k_s)
            vsm.append(v_s)
        xp = xp + op
        xs = xs + os_
        hp = rms_norm(xp, norm_w[layer, 1])
        hs = rms_norm(xs, norm_w[layer, 1])
        if layer % 2 == 0:
            xp = xp + swiglu(hp, ffn_w_gate[li], ffn_w_up[li], ffn_w_down[li])
            xs = xs + swiglu(hs, ffn_w_gate[li], ffn_w_up[li], ffn_w_down[li])
        else:
            xp = xp + moe_swiglu(hp, moe_w_router[li], moe_w_gate[li], moe_w_up[li], moe_w_down[li])
            xs = xs + moe_swiglu(hs, moe_w_router[li], moe_w_gate[li], moe_w_up[li], moe_w_down[li])
    y_prompt = rms_norm(xp, final_norm_w)
    y_sample = rms_norm(xs, final_norm_w)
    return (y_prompt, y_sample, jnp.stack(ret_p), jnp.stack(ret_s),
            jnp.stack(kp), jnp.stack(vp), jnp.stack(ksm), jnp.stack(vsm))
```

```python
import functools

import jax
import jax.numpy as jnp
from jax import lax
from jax.experimental import pallas as pl
from jax.experimental.pallas import tpu as pltpu

RET_CHUNK = 128
ROPE_BASE = 10000.0
MOBA_BLOCK = 256
MOBA_TOPK = 3
MOE_TOPK = 2
RMS_EPS = 1e-6
GN_EPS = 1e-5

V7X_VMEM_LIMIT_BYTES = 56 * 1024 * 1024
ROW_ALIGN = 256
NEG_BIG = -1e30

F32 = jnp.float32
BF16 = jnp.bfloat16


def _cparams(*sem):
    return pltpu.CompilerParams(dimension_semantics=sem, vmem_limit_bytes=V7X_VMEM_LIMIT_BYTES)


def _pick(n, prefs):
    for p in prefs:
        if p <= n and n % p == 0:
            return p
    return n


def _silu(g):
    return g * (1.0 / (1.0 + jnp.exp(-g)))


def _dot(a, b):
    return jnp.dot(a, b, preferred_element_type=F32)


def _dot_nt(a, b, precision=None):
    return lax.dot_general(a, b, (((1,), (1,)), ((), ())), precision=precision,
                           preferred_element_type=F32)


def _rmsnorm_kernel(x_ref, w_ref, o_ref):
    x = x_ref[...]
    y = x * lax.rsqrt(jnp.mean(x * x, axis=-1, keepdims=True) + RMS_EPS)
    o_ref[...] = (y * w_ref[...]).astype(o_ref.dtype)


def rmsnorm(x, w, out_dtype):
    m, d = x.shape
    tm = _pick(m, (256, 128, 64, 32, 16, 8))
    return pl.pallas_call(
        _rmsnorm_kernel,
        grid=(m // tm,),
        in_specs=[pl.BlockSpec((tm, d), lambda i: (i, 0)), pl.BlockSpec((1, d), lambda i: (0, 0))],
        out_specs=pl.BlockSpec((tm, d), lambda i: (i, 0)),
        out_shape=jax.ShapeDtypeStruct((m, d), out_dtype),
        compiler_params=_cparams("parallel"),
        name="rmsnorm",
    )(x, w.reshape(1, d))


def _mm_kernel(*refs, has_res):
    if has_res:
        x_ref, w_ref, r_ref, o_ref, wb_ref = refs
    else:
        x_ref, w_ref, o_ref, wb_ref = refs

    @pl.when(pl.program_id(1) == 0)
    def _():
        wb_ref[...] = w_ref[...].astype(BF16)

    acc = _dot(x_ref[...], wb_ref[...])
    if has_res:
        acc = acc + r_ref[...]
    o_ref[...] = acc.astype(o_ref.dtype)


def matmul(x, w3, layer, *, name, n_off=0, n=None, res=None, out_dtype=F32, tn_prefs=(512, 256, 128)):
    m, k = x.shape
    n = w3.shape[2] if n is None else n
    tm = _pick(m, (528, 256, 128, 64, 32, 16))
    tn = _pick(n, tn_prefs)
    assert n_off % tn == 0
    off = n_off // tn
    in_specs = [pl.BlockSpec((tm, k), lambda j, i: (i, 0)),
                pl.BlockSpec((None, k, tn), lambda j, i: (layer, 0, j + off))]
    args = [x, w3]
    if res is not None:
        in_specs.append(pl.BlockSpec((tm, tn), lambda j, i: (i, j)))
        args.append(res)
    return pl.pallas_call(
        functools.partial(_mm_kernel, has_res=res is not None),
        grid=(n // tn, m // tm),
        in_specs=in_specs,
        out_specs=pl.BlockSpec((tm, tn), lambda j, i: (i, j)),
        out_shape=jax.ShapeDtypeStruct((m, n), out_dtype),
        scratch_shapes=[pltpu.VMEM((k, tn), BF16)],
        compiler_params=_cparams("arbitrary", "arbitrary"),
        name=name,
    )(*args)


def _gateup_kernel(x_ref, wg_ref, wu_ref, o_ref, wgb_ref, wub_ref):
    @pl.when(pl.program_id(1) == 0)
    def _():
        wgb_ref[...] = wg_ref[...].astype(BF16)
        wub_ref[...] = wu_ref[...].astype(BF16)

    x = x_ref[...]
    g = _dot(x, wgb_ref[...])
    u = _dot(x, wub_ref[...])
    o_ref[...] = (_silu(g) * u).astype(o_ref.dtype)


def gate_up(x, wg3, wu3, layer):
    m, k = x.shape
    f = wg3.shape[2]
    tm = _pick(m, (528, 256, 128, 64, 32, 16))
    tn = _pick(f, (512, 256, 128))
    wspec = pl.BlockSpec((None, k, tn), lambda j, i: (layer, 0, j))
    return pl.pallas_call(
        _gateup_kernel,
        grid=(f // tn, m // tm),
        in_specs=[pl.BlockSpec((tm, k), lambda j, i: (i, 0)), wspec, wspec],
        out_specs=pl.BlockSpec((tm, tn), lambda j, i: (i, j)),
        out_shape=jax.ShapeDtypeStruct((m, f), BF16),
        scratch_shapes=[pltpu.VMEM((k, tn), BF16), pltpu.VMEM((k, tn), BF16)],
        compiler_params=_cparams("arbitrary", "arbitrary"),
        name="gate_up",
    )(x, wg3, wu3)


def _retention_kernel(q_ref, k_ref, v_ref, g_ref, cos_ref, sin_ref, dec_ref, qd_ref, kd_ref,
                      cd_ref, s0_ref, y_ref, sout_ref, state_ref, *, n_chunks, dk):
    c = pl.program_id(2)

    @pl.when(c == 0)
    def _():
        state_ref[...] = s0_ref[0, 0]

    half = dk // 2
    cos = cos_ref[...]
    sin = sin_ref[...]

    def rot(x):
        x1, x2 = x[:, :half], x[:, half:]
        return jnp.concatenate([x1 * cos - x2 * sin, x1 * sin + x2 * cos], axis=-1)

    q = rot(q_ref[...])
    k = rot(k_ref[...]) * (dk ** -0.5)
    qb = q.astype(BF16)
    kb = k.astype(BF16)
    vb = v_ref[...].astype(BF16)
    scores = _dot_nt(qb, kb) * dec_ref[0]
    inner = _dot(scores.astype(BF16), vb)
    st = state_ref[...]
    cross = _dot(qb, st.astype(BF16)) * qd_ref[0]
    kdt = jnp.transpose((k * kd_ref[0]).astype(F32)).astype(BF16)
    state_ref[...] = st * cd_ref[0] + _dot(kdt, vb)
    y = inner + cross
    mu = jnp.mean(y, axis=-1, keepdims=True)
    d = y - mu
    var = jnp.mean(d * d, axis=-1, keepdims=True)
    yn = d * lax.rsqrt(var + GN_EPS)
    y_ref[...] = (_silu(g_ref[...]) * yn).astype(y_ref.dtype)

    @pl.when(c == n_chunks - 1)
    def _():
        sout_ref[0, 0] = state_ref[...]


def _retention_tables(heads, chunk, valid):
    lg = jnp.log1p(-jnp.exp2(-5.0 - jnp.arange(heads, dtype=F32)))
    i = jnp.arange(chunk, dtype=F32)
    ok = i < valid
    diff = i[:, None] - i[None, :]
    dec = jnp.where((diff[None] >= 0) & ok[None, :, None] & ok[None, None, :],
                    jnp.exp(jnp.maximum(diff, 0.0)[None] * lg[:, None, None]), 0.0)
    qd = jnp.where(ok[None, :], jnp.exp((i + 1.0)[None, :] * lg[:, None]), 0.0)
    kd = jnp.where(ok[None, :], jnp.exp((valid - 1.0 - i)[None, :] * lg[:, None]), 0.0)
    cd = jnp.exp(valid * lg)
    return dec, qd[:, :, None], kd[:, :, None], cd[:, None, None]


def _rope_tables(pos, half):
    inv = ROPE_BASE ** (-jnp.arange(half, dtype=F32) / half)
    ang = pos.astype(F32)[:, None] * inv[None, :]
    return jnp.cos(ang), jnp.sin(ang)


def retention(qkvg, state0, pos, *, batch, n_chunks, chunk, valid, heads, dk, dv):
    rows = batch * n_chunks * chunk
    assert qkvg.shape[0] >= rows
    dec, qd, kd, cd = _retention_tables(heads, chunk, valid)
    cos, sin = _rope_tables(pos, dk // 2)
    kq = heads
    kv = 2 * heads * dk // dv
    kg = kv + heads
    row = lambda b, h, c: b * n_chunks + c
    in_specs = [
        pl.BlockSpec((chunk, dk), lambda b, h, c: (row(b, h, c), h)),
        pl.BlockSpec((chunk, dk), lambda b, h, c: (row(b, h, c), kq + h)),
        pl.BlockSpec((chunk, dv), lambda b, h, c: (row(b, h, c), kv + h)),
        pl.BlockSpec((chunk, dv), lambda b, h, c: (row(b, h, c), kg + h)),
        pl.BlockSpec((chunk, dk // 2), lambda b, h, c: (c, 0)),
        pl.BlockSpec((chunk, dk // 2), lambda b, h, c: (c, 0)),
        pl.BlockSpec((1, chunk, chunk), lambda b, h, c: (h, 0, 0)),
        pl.BlockSpec((1, chunk, 1), lambda b, h, c: (h, 0, 0)),
        pl.BlockSpec((1, chunk, 1), lambda b, h, c: (h, 0, 0)),
        pl.BlockSpec((1, 1, 1), lambda b, h, c: (h, 0, 0)),
        pl.BlockSpec((1, 1, dk, dv), lambda b, h, c: (b, h, 0, 0)),
    ]
    out_specs = [
        pl.BlockSpec((chunk, dv), lambda b, h, c: (row(b, h, c), h)),
        pl.BlockSpec((1, 1, dk, dv), lambda b, h, c: (b, h, 0, 0)),
    ]
    return pl.pallas_call(
        functools.partial(_retention_kernel, n_chunks=n_chunks, dk=dk),
        grid=(batch, heads, n_chunks),
        in_specs=in_specs,
        out_specs=out_specs,
        out_shape=[jax.ShapeDtypeStruct((rows, heads * dv), BF16),
                   jax.ShapeDtypeStruct((batch, heads, dk, dv), F32)],
        scratch_shapes=[pltpu.VMEM((dk, dv), F32)],
        compiler_params=_cparams("arbitrary", "arbitrary", "arbitrary"),
        name="retention",
    )(qkvg, qkvg, qkvg, qkvg, cos, sin, dec, qd, kd, cd, state0)


def _top_select(gate, valid, n_sel):
    r, n = gate.shape
    col = lax.broadcasted_iota(jnp.int32, (r, n), 1)
    g = jnp.where(valid, gate, -jnp.inf)
    sel = jnp.zeros((r, n), F32)
    for _ in range(n_sel):
        m = jnp.max(g, axis=-1, keepdims=True)
        idx = jnp.min(jnp.where(g == m, col, n), axis=-1, keepdims=True)
        pick = (col == idx) & (m > -jnp.inf)
        sel = jnp.where(pick, 1.0, sel)
        g = jnp.where(pick, -jnp.inf, g)
    return sel


def _moba_prompt_kernel(q_ref, k_ref, v_ref, o_ref, kb_ref, vb_ref, means_ref, *, n_blocks, hd):
    t = pl.program_id(2)
    blk = MOBA_BLOCK

    @pl.when(t == 0)
    def _():
        kb_ref[...] = k_ref[...].astype(BF16)
        vb_ref[...] = v_ref[...].astype(BF16)
        for n in range(n_blocks):
            means_ref[n:n + 1, :] = jnp.mean(k_ref[n * blk:(n + 1) * blk, :], axis=0, keepdims=True)

    q = q_ref[...]
    qb = q.astype(BF16)
    scale = hd ** -0.5
    gate = _dot_nt(q, means_ref[...], precision=lax.Precision.HIGHEST)
    bcol = lax.broadcasted_iota(jnp.int32, (blk, n_blocks), 1)
    sel = _top_select(gate, bcol < t, min(MOBA_TOPK, n_blocks))

    r0 = pl.multiple_of(t * blk, blk)
    qi = lax.broadcasted_iota(jnp.int32, (blk, blk), 0)
    ki = lax.broadcasted_iota(jnp.int32, (blk, blk), 1)
    s = _dot_nt(qb, kb_ref[pl.ds(r0, blk), :]) * scale
    s = jnp.where(ki <= qi, s, NEG_BIG)
    m0 = jnp.max(s, axis=-1, keepdims=True)
    p = jnp.exp(s - m0)
    l0 = jnp.sum(p, axis=-1, keepdims=True)
    acc0 = _dot(p.astype(BF16), vb_ref[pl.ds(r0, blk), :])

    def body(n, carry):
        m, l, acc = carry
        rn = pl.multiple_of(n * blk, blk)
        seln = jnp.sum(jnp.where(bcol == n, sel, 0.0), axis=-1, keepdims=True)
        s = _dot_nt(qb, kb_ref[pl.ds(rn, blk), :]) * scale
        s = jnp.where(seln > 0.5, s, NEG_BIG)
        m_new = jnp.maximum(m, jnp.max(s, axis=-1, keepdims=True))
        alpha = jnp.exp(m - m_new)
        p = jnp.exp(s - m_new)
        l = alpha * l + jnp.sum(p, axis=-1, keepdims=True)
        acc = alpha * acc + _dot(p.astype(BF16), vb_ref[pl.ds(rn, blk), :])
        return m_new, l, acc

    m, l, acc = lax.fori_loop(0, t, body, (m0, l0, acc0))
    o_ref[...] = (acc / l).astype(o_ref.dtype)


def moba_prompt(q, k, v, *, batch, seq, heads, hd):
    n_blocks = seq // MOBA_BLOCK
    assert seq % MOBA_BLOCK == 0
    qspec = pl.BlockSpec((MOBA_BLOCK, hd), lambda b, h, t: (b * n_blocks + t, h))
    kvspec = pl.BlockSpec((seq, hd), lambda b, h, t: (b, h))
    return pl.pallas_call(
        functools.partial(_moba_prompt_kernel, n_blocks=n_blocks, hd=hd),
        grid=(batch, heads, n_blocks),
        in_specs=[qspec, kvspec, kvspec],
        out_specs=qspec,
        out_shape=jax.ShapeDtypeStruct((batch * seq, heads * hd), BF16),
        scratch_shapes=[pltpu.VMEM((seq, hd), BF16), pltpu.VMEM((seq, hd), BF16),
                        pltpu.VMEM((n_blocks, hd), F32)],
        compiler_params=_cparams("arbitrary", "arbitrary", "arbitrary"),
        name="moba_prompt",
    )(q, k, v)


def _page_means_kernel(pt_ref, kp_ref, o_ref, *, pages_per_block):
    j = pl.program_id(1)
    n = j // pages_per_block
    s = jnp.sum(kp_ref[0], axis=0, keepdims=True)

    @pl.when(j % pages_per_block == 0)
    def _():
        o_ref[0, pl.ds(n, 1), :] = s

    @pl.when(j % pages_per_block != 0)
    def _():
        o_ref[0, pl.ds(n, 1), :] = o_ref[0, pl.ds(n, 1), :] + s

    @pl.when(j % pages_per_block == pages_per_block - 1)
    def _():
        o_ref[0, pl.ds(n, 1), :] = o_ref[0, pl.ds(n, 1), :] * (1.0 / MOBA_BLOCK)


def page_block_means(cache_k2, pt_flat, *, batch, n_pages, page, width):
    ppb = MOBA_BLOCK // page
    nb = n_pages // ppb
    return pl.pallas_call(
        functools.partial(_page_means_kernel, pages_per_block=ppb),
        grid_spec=pltpu.PrefetchScalarGridSpec(
            num_scalar_prefetch=1,
            grid=(batch, n_pages),
            in_specs=[pl.BlockSpec((1, page, width), lambda b, j, pt: (pt[b * n_pages + j], 0, 0))],
            out_specs=pl.BlockSpec((1, nb, width), lambda b, j, pt: (b, 0, 0)),
        ),
        out_shape=jax.ShapeDtypeStruct((batch, nb, width), F32),
        compiler_params=_cparams("arbitrary", "arbitrary"),
        name="page_means",
    )(pt_flat, cache_k2)


SAMPLE_ROWS = 8


def _sample_gate_kernel(q_ref, m_ref, idx_ref, *, n_sel):
    gate = _dot_nt(q_ref[...], m_ref[0], precision=lax.Precision.HIGHEST)
    r, n = gate.shape
    col = lax.broadcasted_iota(jnp.int32, (r, n), 1)
    lane = lax.broadcasted_iota(jnp.int32, (r, 128), 1)
    out = jnp.zeros((r, 128), jnp.int32)
    g = gate
    for s in range(n_sel):
        m = jnp.max(g, axis=-1, keepdims=True)
        idx = jnp.min(jnp.where(g == m, col, n), axis=-1, keepdims=True)
        out = jnp.where(lane == s, idx, out)
        g = jnp.where(col == idx, -jnp.inf, g)
    idx_ref[...] = out


def sample_gate(q8, means, *, batch, heads, hd):
    nb = means.shape[1]
    assert nb >= MOBA_TOPK
    return pl.pallas_call(
        functools.partial(_sample_gate_kernel, n_sel=MOBA_TOPK),
        grid=(batch, heads),
        in_specs=[pl.BlockSpec((SAMPLE_ROWS, hd), lambda b, h: (b, h)),
                  pl.BlockSpec((1, nb, hd), lambda b, h: (b, 0, h))],
        out_specs=pl.BlockSpec((SAMPLE_ROWS, 128), lambda b, h: (b * heads + h, 0)),
        out_shape=jax.ShapeDtypeStruct((batch * heads * SAMPLE_ROWS, 128), jnp.int32),
        compiler_params=_cparams("arbitrary", "arbitrary"),
        name="sample_gate",
    )(q8, means)


def _sample_attn_kernel(pp_ref, q_ref, kn_ref, vn_ref, *refs, n_pg, n_tok, hd):
    kp = refs[:n_pg]
    vp = refs[n_pg:2 * n_pg]
    o_ref = refs[2 * n_pg]
    t = pl.program_id(2)
    r = SAMPLE_ROWS
    scale = hd ** -0.5
    qb = q_ref[...].astype(BF16)
    s_pg = [_dot_nt(qb, kp[j][0].astype(BF16)) * scale for j in range(n_pg)]
    s_own = _dot_nt(qb, kn_ref[...].astype(BF16)) * scale
    qi = lax.broadcasted_iota(jnp.int32, (r, r), 0)
    ki = lax.broadcasted_iota(jnp.int32, (r, r), 1)
    s_own = jnp.where((ki <= qi) & (ki < n_tok), s_own, NEG_BIG)
    m = jnp.max(s_own, axis=-1, keepdims=True)
    for s in s_pg:
        m = jnp.maximum(m, jnp.max(s, axis=-1, keepdims=True))
    p_own = jnp.exp(s_own - m)
    l = jnp.sum(p_own, axis=-1, keepdims=True)
    acc = _dot(p_own.astype(BF16), vn_ref[...].astype(BF16))
    for j in range(n_pg):
        p = jnp.exp(s_pg[j] - m)
        l = l + jnp.sum(p, axis=-1, keepdims=True)
        acc = acc + _dot(p.astype(BF16), vp[j][0].astype(BF16))
    out = acc / l
    rows = lax.broadcasted_iota(jnp.int32, (r, hd), 0)

    @pl.when(t == 0)
    def _():
        o_ref[...] = jnp.zeros_like(o_ref)

    o_ref[pl.ds(t, 1), :] = jnp.sum(jnp.where(rows == t, out, 0.0), axis=0, keepdims=True)


def sample_attention(q8, kn8, vn8, cache_k2, cache_v2, pp_flat, *, batch, heads, hd, n_tok, page):
    ppb = MOBA_BLOCK // page
    n_pg = MOBA_TOPK * ppb

    def pspec(j):
        return pl.BlockSpec(
            (1, page, hd),
            lambda b, h, t, pp: (pp[((b * heads + h) * n_tok + t) * n_pg + j], 0, h))

    small = pl.BlockSpec((SAMPLE_ROWS, hd), lambda b, h, t, pp: (b, h))
    return pl.pallas_call(
        functools.partial(_sample_attn_kernel, n_pg=n_pg, n_tok=n_tok, hd=hd),
        grid_spec=pltpu.PrefetchScalarGridSpec(
            num_scalar_prefetch=1,
            grid=(batch, heads, n_tok),
            in_specs=[small, small, small] + [pspec(j) for j in range(n_pg)] * 2,
            out_specs=small,
        ),
        out_shape=jax.ShapeDtypeStruct((batch * SAMPLE_ROWS, heads * hd), F32),
        compiler_params=_cparams("arbitrary", "arbitrary", "arbitrary"),
        name="sample_attn",
    )(pp_flat, q8, kn8, vn8, *([cache_k2] * n_pg), *([cache_v2] * n_pg))


def _norm_router_kernel(x_ref, w_ref, wr_ref, h_ref, info_ref, *, n_experts):
    x = x_ref[...]
    y = x * lax.rsqrt(jnp.mean(x * x, axis=-1, keepdims=True) + RMS_EPS) * w_ref[...]
    h_ref[...] = y
    logits = jnp.dot(y, wr_ref[...], precision=lax.Precision.HIGHEST, preferred_element_type=F32)
    r, n = logits.shape
    col = lax.broadcasted_iota(jnp.int32, (r, n), 1)
    lg = jnp.where(col < n_experts, logits, -jnp.inf)
    m1 = jnp.max(lg, axis=-1, keepdims=True)
    i1 = jnp.min(jnp.where(lg == m1, col, n), axis=-1, keepdims=True)
    lg2 = jnp.where(col == i1, -jnp.inf, lg)
    m2 = jnp.max(lg2, axis=-1, keepdims=True)
    i2 = jnp.min(jnp.where(lg2 == m2, col, n), axis=-1, keepdims=True)
    e2 = jnp.exp(m2 - m1)
    den = 1.0 + e2
    w1 = 1.0 / den
    w2 = e2 / den
    info = jnp.where(col == 0, i1.astype(F32),
                     jnp.where(col == 1, i2.astype(F32),
                               jnp.where(col == 2, w1, jnp.where(col == 3, w2, 0.0))))
    info_ref[...] = info


def norm_router(x, w, w_router):
    m, d = x.shape
    e = w_router.shape[1]
    tm = _pick(m, (256, 128, 64, 32, 16, 8))
    wr = jnp.zeros((d, 128), F32).at[:, :e].set(w_router)
    return pl.pallas_call(
        functools.partial(_norm_router_kernel, n_experts=e),
        grid=(m // tm,),
        in_specs=[pl.BlockSpec((tm, d), lambda i: (i, 0)), pl.BlockSpec((1, d), lambda i: (0, 0)),
                  pl.BlockSpec((d, 128), lambda i: (0, 0))],
        out_specs=[pl.BlockSpec((tm, d), lambda i: (i, 0)), pl.BlockSpec((tm, 128), lambda i: (i, 0))],
        out_shape=[jax.ShapeDtypeStruct((m, d), F32), jax.ShapeDtypeStruct((m, 128), F32)],
        compiler_params=_cparams("parallel"),
        name="norm_router",
    )(x, w.reshape(1, d), wr)


GATHER_TILE = 256


def _gather_rows(src_hbm, dst_ref, sem, row_of, n_rows):
    def copy(r):
        return pltpu.make_async_copy(src_hbm.at[pl.ds(row_of(r), 1), :], dst_ref.at[pl.ds(r, 1), :], sem)

    def start(r, c):
        copy(r).start()
        return c

    def wait(r, c):
        copy(r).wait()
        return c

    lax.fori_loop(0, n_rows, start, 0)
    lax.fori_loop(0, n_rows, wait, 0)


def _gather_cast_kernel(src_ref, ntile_ref, h_hbm, o_ref, buf_ref, sem):
    i = pl.program_id(0)

    @pl.when(i < ntile_ref[0])
    def _():
        base = i * GATHER_TILE
        _gather_rows(h_hbm, buf_ref, sem, lambda r: src_ref[base + r], GATHER_TILE)
        o_ref[...] = buf_ref[...].astype(o_ref.dtype)

    @pl.when(i >= ntile_ref[0])
    def _():
        o_ref[...] = jnp.zeros_like(o_ref)


def gather_cast(h, src_rows, n_tiles_used):
    s_rows = src_rows.shape[0]
    d = h.shape[1]
    return pl.pallas_call(
        _gather_cast_kernel,
        grid_spec=pltpu.PrefetchScalarGridSpec(
            num_scalar_prefetch=2,
            grid=(s_rows // GATHER_TILE,),
            in_specs=[pl.BlockSpec(memory_space=pl.ANY)],
            out_specs=pl.BlockSpec((GATHER_TILE, d), lambda i, s, n: (i, 0)),
            scratch_shapes=[pltpu.VMEM((GATHER_TILE, d), F32), pltpu.SemaphoreType.DMA(())],
        ),
        out_shape=jax.ShapeDtypeStruct((s_rows, d), BF16),
        compiler_params=_cparams("arbitrary"),
        name="gather_cast",
    )(src_rows, n_tiles_used, h)


EXPERT_SUB = 256
SEGMENT_ROWS = 2304


def _experts_kernel(seg_e_ref, seg_rows_ref, nseg_ref, x_ref, wg_ref, wu_ref, wd_ref, o_ref,
                    wgb_ref, wub_ref, wdb_ref):
    s = pl.program_id(0)
    f = pl.program_id(1)

    @pl.when(f == 0)
    def _():
        o_ref[...] = jnp.zeros_like(o_ref)

    @pl.when(s < nseg_ref[0])
    def _():
        wgb_ref[...] = wg_ref[...].astype(BF16)
        wub_ref[...] = wu_ref[...].astype(BF16)
        wdb_ref[...] = wd_ref[...].astype(BF16)
        n_sub = (seg_rows_ref[s] + EXPERT_SUB - 1) // EXPERT_SUB

        def body(i, c):
            r0 = pl.multiple_of(i * EXPERT_SUB, EXPERT_SUB)
            xs = x_ref[pl.ds(r0, EXPERT_SUB), :]
            g = _dot(xs, wgb_ref[...])
            u = _dot(xs, wub_ref[...])
            a = (_silu(g) * u).astype(BF16)
            o_ref[pl.ds(r0, EXPERT_SUB), :] += _dot(a, wdb_ref[...])
            return c

        lax.fori_loop(0, n_sub, body, 0)


def experts(x_sorted, wg, wu, wd, layer, seg_e, seg_rows, nseg, *, seg_cap):
    s_rows, d = x_sorted.shape
    s_max = s_rows // seg_cap
    f_dim = wg.shape[3]
    tf = _pick(f_dim, (256, 128))
    nf = f_dim // tf

    def seg_blk(s, n):
        return jnp.minimum(s, n[0] - 1)

    def fblk(s, f, n):
        return jnp.where(s < n[0], f, nf - 1)

    one = pl.Buffered(1)
    in_specs = [
        pl.BlockSpec((seg_cap, d), lambda s, f, e, r, n: (seg_blk(s, n), 0), pipeline_mode=one),
        pl.BlockSpec((None, None, d, tf), lambda s, f, e, r, n: (layer, e[s], 0, fblk(s, f, n))),
        pl.BlockSpec((None, None, d, tf), lambda s, f, e, r, n: (layer, e[s], 0, fblk(s, f, n))),
        pl.BlockSpec((None, None, tf, d), lambda s, f, e, r, n: (layer, e[s], fblk(s, f, n), 0)),
    ]
    return pl.pallas_call(
        _experts_kernel,
        grid_spec=pltpu.PrefetchScalarGridSpec(
            num_scalar_prefetch=3,
            grid=(s_max, nf),
            in_specs=in_specs,
            out_specs=pl.BlockSpec((seg_cap, d), lambda s, f, e, r, n: (s, 0), pipeline_mode=one),
            scratch_shapes=[pltpu.VMEM((d, tf), BF16), pltpu.VMEM((d, tf), BF16),
                            pltpu.VMEM((tf, d), BF16)],
        ),
        out_shape=jax.ShapeDtypeStruct((s_rows, d), F32),
        compiler_params=_cparams("arbitrary", "arbitrary"),
        name="experts",
    )(seg_e, seg_rows, nseg, x_sorted, wg, wu, wd)


def _combine_norm_kernel(s1_ref, s2_ref, ys_hbm, x_ref, info_ref, w_ref, o_ref, a_ref, b_ref, sem):
    i = pl.program_id(0)
    tm = x_ref.shape[0]
    base = i * tm
    _gather_rows(ys_hbm, a_ref, sem.at[0], lambda r: s1_ref[base + r], tm)
    _gather_rows(ys_hbm, b_ref, sem.at[1], lambda r: s2_ref[base + r], tm)
    info = info_ref[...]
    y = info[:, 2:3] * a_ref[...] + info[:, 3:4] * b_ref[...]
    x = x_ref[...] + y
    xn = x * lax.rsqrt(jnp.mean(x * x, axis=-1, keepdims=True) + RMS_EPS)
    o_ref[...] = xn * w_ref[...]


def combine_norm(x, y_sorted, slot1, slot2, info, w):
    m, d = x.shape
    tm = _pick(m, (256, 128, 64, 32, 16, 8))
    return pl.pallas_call(
        _combine_norm_kernel,
        grid_spec=pltpu.PrefetchScalarGridSpec(
            num_scalar_prefetch=2,
            grid=(m // tm,),
            in_specs=[pl.BlockSpec(memory_space=pl.ANY),
                      pl.BlockSpec((tm, d), lambda i, a, b: (i, 0)),
                      pl.BlockSpec((tm, 128), lambda i, a, b: (i, 0)),
                      pl.BlockSpec((1, d), lambda i, a, b: (0, 0))],
            out_specs=pl.BlockSpec((tm, d), lambda i, a, b: (i, 0)),
            scratch_shapes=[pltpu.VMEM((tm, d), F32), pltpu.VMEM((tm, d), F32),
                            pltpu.SemaphoreType.DMA((2,))],
        ),
        out_shape=jax.ShapeDtypeStruct((m, d), F32),
        compiler_params=_cparams("arbitrary"),
        name="combine_norm",
    )(slot1, slot2, y_sorted, x, info, w.reshape(1, d))


def _route_tables(info, n_real, n_experts, seg_cap):
    m = info.shape[0]
    e = info[:, :MOE_TOPK].astype(jnp.int32)
    valid = (jnp.arange(m) < n_real)[:, None]
    onehot = ((e[:, :, None] == jnp.arange(n_experts)[None, None, :]) & valid[:, :, None])
    onehot = onehot.reshape(m * MOE_TOPK, n_experts).astype(jnp.int32)
    rank = jnp.sum((jnp.cumsum(onehot, axis=0) - onehot) * onehot, axis=1)
    counts = jnp.sum(onehot, axis=0)
    nseg_e = (counts + seg_cap - 1) // seg_cap
    seg_start = jnp.cumsum(nseg_e) - nseg_e
    nseg = jnp.sum(nseg_e)
    s_max = (m * MOE_TOPK) // seg_cap + n_experts
    flat_e = e.reshape(-1)
    flat_valid = jnp.broadcast_to(valid, (m, MOE_TOPK)).reshape(-1)
    slot = (seg_start[flat_e] + rank // seg_cap) * seg_cap + rank % seg_cap
    slot = jnp.where(flat_valid, slot, 0)
    sid = jnp.arange(s_max)
    owner = jnp.argmax((sid[:, None] >= seg_start[None, :]) & (sid[:, None] < (seg_start + nseg_e)[None, :]), axis=1)
    seg_e = jnp.where(sid < nseg, owner, owner[jnp.maximum(nseg - 1, 0)]).astype(jnp.int32)
    within = sid - seg_start[seg_e]
    seg_rows = jnp.clip(counts[seg_e] - within * seg_cap, 0, seg_cap)
    seg_rows = jnp.where(sid < nseg, seg_rows, 0).astype(jnp.int32)
    tok = jnp.repeat(jnp.arange(m, dtype=jnp.int32), MOE_TOPK)
    src = jnp.full((s_max * seg_cap,), m - 1, jnp.int32)
    src = src.at[jnp.where(flat_valid, slot, s_max * seg_cap)].set(tok, mode="drop")
    n_tiles_used = (nseg * (seg_cap // GATHER_TILE)).astype(jnp.int32).reshape(1)
    slots = slot.reshape(m, MOE_TOPK).astype(jnp.int32)
    return src, n_tiles_used, seg_e, seg_rows, nseg.astype(jnp.int32).reshape(1), slots[:, 0], slots[:, 1]


def kernel(x_prompt, x_sample, state_ret, cache_k, cache_v, page_table, norm_w, final_norm_w,
           ret_w_in, ret_w_out, moba_w_qkv, moba_w_out, ffn_w_gate, ffn_w_up, ffn_w_down,
           moe_w_router, moe_w_gate, moe_w_up, moe_w_down):
    bp, sp, d = x_prompt.shape
    bs, ts, _ = x_sample.shape
    _, _, r_heads, dk, dv = state_ret.shape
    _, n_pool, page, m_heads, hd = cache_k.shape
    n_pages = page_table.shape[1]
    past_len = n_pages * page
    n_experts = moe_w_router.shape[2]
    assert sp % RET_CHUNK == 0 and ts < RET_CHUNK and ts <= SAMPLE_ROWS
    assert MOBA_BLOCK % page == 0 and past_len % MOBA_BLOCK == 0 and sp % MOBA_BLOCK == 0

    n_p = bp * sp
    n_s = bs * ts
    n_real = n_p + n_s
    m = -(-(n_real + 1) // ROW_ALIGN) * ROW_ALIGN
    x = jnp.concatenate([x_prompt.reshape(n_p, d), x_sample.reshape(n_s, d),
                         jnp.zeros((m - n_real, d), F32)], axis=0)

    h = rmsnorm(x, norm_w[0, 0], BF16)
    qkvg = matmul(h, ret_w_in, 0, name="ret_in_proj")
    pos_p = jnp.arange(sp, dtype=jnp.int32)
    zero_state = jnp.zeros((bp, r_heads, dk, dv), F32)
    y_p, st_p = retention(qkvg, zero_state, pos_p, batch=bp, n_chunks=sp // RET_CHUNK,
                          chunk=RET_CHUNK, valid=RET_CHUNK, heads=r_heads, dk=dk, dv=dv)
    qkvg_s = jnp.pad(qkvg[n_p:n_real].reshape(bs, ts, -1), ((0, 0), (0, RET_CHUNK - ts), (0, 0)))
    pos_s = past_len + jnp.arange(RET_CHUNK, dtype=jnp.int32)
    y_s, st_s = retention(qkvg_s.reshape(bs * RET_CHUNK, -1), state_ret[0], pos_s, batch=bs, n_chunks=1,
                          chunk=RET_CHUNK, valid=ts, heads=r_heads, dk=dk, dv=dv)
    y_s = y_s.reshape(bs, RET_CHUNK, -1)[:, :ts].reshape(n_s, -1)
    y = jnp.concatenate([y_p[:n_p], y_s, jnp.zeros((m - n_real, y_s.shape[1]), BF16)], axis=0)
    x = matmul(y, ret_w_out, 0, res=x, name="ret_out_proj")

    h = rmsnorm(x, norm_w[0, 1], BF16)
    a = gate_up(h, ffn_w_gate, ffn_w_up, 0)
    x = matmul(a, ffn_w_down, 0, res=x, tn_prefs=(256, 128), name="ffn_down")

    h = rmsnorm(x, norm_w[1, 0], BF16)
    q = matmul(h, moba_w_qkv, 0, n_off=0, n=d, name="moba_q_proj")
    k = matmul(h, moba_w_qkv, 0, n_off=d, n=d, name="moba_k_proj")
    v = matmul(h, moba_w_qkv, 0, n_off=2 * d, n=d, name="moba_v_proj")
    o_p = moba_prompt(q, k, v, batch=bp, seq=sp, heads=m_heads, hd=hd)

    def pad8(a2):
        return jnp.pad(a2[n_p:n_real].reshape(bs, ts, d),
                       ((0, 0), (0, SAMPLE_ROWS - ts), (0, 0))).reshape(bs * SAMPLE_ROWS, d)

    q8, k8, v8 = pad8(q), pad8(k), pad8(v)
    cache_k2 = cache_k[0].reshape(n_pool, page, m_heads * hd)
    cache_v2 = cache_v[0].reshape(n_pool, page, m_heads * hd)
    means = page_block_means(cache_k2, page_table.reshape(-1), batch=bs, n_pages=n_pages, page=page,
                             width=m_heads * hd)
    sel = sample_gate(q8, means, batch=bs, heads=m_heads, hd=hd)
    sel = sel.reshape(bs, m_heads, SAMPLE_ROWS, 128)[:, :, :ts, :MOBA_TOPK]
    ppb = MOBA_BLOCK // page
    lpage = sel[..., None] * ppb + jnp.arange(ppb, dtype=jnp.int32)
    pp = page_table[jnp.arange(bs)[:, None, None, None], lpage.reshape(bs, m_heads, ts, MOBA_TOPK * ppb)]
    o_s8 = sample_attention(q8, k8, v8, cache_k2, cache_v2, pp.reshape(-1).astype(jnp.int32),
                            batch=bs, heads=m_heads, hd=hd, n_tok=ts, page=page)
    o_s = o_s8.reshape(bs, SAMPLE_ROWS, d)[:, :ts].reshape(n_s, d).astype(BF16)
    o = jnp.concatenate([o_p, o_s, jnp.zeros((m - n_real, d), BF16)], axis=0)
    x = matmul(o, moba_w_out, 0, res=x, name="moba_out_proj")

    hf, info = norm_router(x, norm_w[1, 1], moe_w_router[0])
    seg_cap = SEGMENT_ROWS if m * MOE_TOPK >= 4 * SEGMENT_ROWS else 2 * GATHER_TILE
    src, n_tiles_used, seg_e, seg_rows, nseg, slot1, slot2 = _route_tables(info, n_real, n_experts, seg_cap)
    x_sorted = gather_cast(hf, src, n_tiles_used)
    y_sorted = experts(x_sorted, moe_w_gate, moe_w_up, moe_w_down, 0, seg_e, seg_rows, nseg, seg_cap=seg_cap)
    y = combine_norm(x, y_sorted, slot1, slot2, info, final_norm_w)

    y_prompt = y[:n_p].reshape(bp, sp, d)
    y_sample = y[n_p:n_real].reshape(bs, ts, d)
    kv_shape_p = (1, bp, sp, m_heads, hd)
    kv_shape_s = (1, bs, ts, m_heads, hd)
    return (y_prompt, y_sample, st_p[None], st_s[None],
            k[:n_p].reshape(kv_shape_p), v[:n_p].reshape(kv_shape_p),
            k[n_p:n_real].reshape(kv_shape_s), v[n_p:n_real].reshape(kv_shape_s))
```

```python
import functools

import jax
import jax.numpy as jnp
from jax import lax
from jax.experimental import pallas as pl
from jax.experimental.pallas import tpu as pltpu

RET_CHUNK = 128
ROPE_BASE = 10000.0
MOBA_BLOCK = 256
MOBA_TOPK = 3
MOE_TOPK = 2
RMS_EPS = 1e-6
GN_EPS = 1e-5

V7X_VMEM_LIMIT_BYTES = 56 * 1024 * 1024
ROW_ALIGN = 256
NEG_BIG = -1e30

F32 = jnp.float32
BF16 = jnp.bfloat16


def _cparams(*sem):
    return pltpu.CompilerParams(dimension_semantics=sem, vmem_limit_bytes=V7X_VMEM_LIMIT_BYTES)


def _pick(n, prefs):
    for p in prefs:
        if p <= n and n % p == 0:
            return p
    return n


def _silu(g):
    return g * (1.0 / (1.0 + jnp.exp(-g)))


def _dot(a, b):
    return jnp.dot(a, b, preferred_element_type=F32)


def _dot_nt(a, b, precision=None):
    return lax.dot_general(a, b, (((1,), (1,)), ((), ())), precision=precision,
                           preferred_element_type=F32)


def _rmsnorm_kernel(x_ref, w_ref, o_ref):
    x = x_ref[...]
    y = x * lax.rsqrt(jnp.mean(x * x, axis=-1, keepdims=True) + RMS_EPS)
    o_ref[...] = (y * w_ref[...]).astype(o_ref.dtype)


def rmsnorm(x, w, out_dtype):
    m, d = x.shape
    tm = _pick(m, (256, 128, 64, 32, 16, 8))
    return pl.pallas_call(
        _rmsnorm_kernel,
        grid=(m // tm,),
        in_specs=[pl.BlockSpec((tm, d), lambda i: (i, 0)), pl.BlockSpec((1, d), lambda i: (0, 0))],
        out_specs=pl.BlockSpec((tm, d), lambda i: (i, 0)),
        out_shape=jax.ShapeDtypeStruct((m, d), out_dtype),
        compiler_params=_cparams("parallel"),
        name="rmsnorm",
    )(x, w.reshape(1, d))


def _mm_kernel(*refs, has_res):
    if has_res:
        x_ref, w_ref, r_ref, o_ref, wb_ref = refs
    else:
        x_ref, w_ref, o_ref, wb_ref = refs

    @pl.when(pl.program_id(1) == 0)
    def _():
        wb_ref[...] = w_ref[...].astype(BF16)

    acc = _dot(x_ref[...], wb_ref[...])
    if has_res:
        acc = acc + r_ref[...]
    o_ref[...] = acc.astype(o_ref.dtype)


TILE_VMEM_BUDGET = 44 * 1024 * 1024


def _matmul_tiles(m, k, n, n_w, out_bytes, has_res):
    best = None
    for tm in (1056, 528, 256, 128, 64, 32, 16):
        if m % tm:
            continue
        for tn in (1024, 512, 256, 128):
            if n % tn:
                continue
            need = (2 * tm * k * 2 + n_w * (2 * k * tn * 4 + k * tn * 2)
                    + 2 * tm * tn * out_bytes + (2 * tm * tn * 4 if has_res else 0))
            if need <= TILE_VMEM_BUDGET and (best is None or tm * tn > best[0] * best[1]):
                best = (tm, tn)
    assert best is not None
    return best


def matmul(x, w3, layer, *, name, n_off=0, n=None, res=None, out_dtype=F32):
    m, k = x.shape
    n = w3.shape[2] if n is None else n
    tm, tn = _matmul_tiles(m, k, n, 1, jnp.dtype(out_dtype).itemsize, res is not None)
    assert n_off % tn == 0
    off = n_off // tn
    in_specs = [pl.BlockSpec((tm, k), lambda j, i: (i, 0)),
                pl.BlockSpec((None, k, tn), lambda j, i: (layer, 0, j + off))]
    args = [x, w3]
    if res is not None:
        in_specs.append(pl.BlockSpec((tm, tn), lambda j, i: (i, j)))
        args.append(res)
    return pl.pallas_call(
        functools.partial(_mm_kernel, has_res=res is not None),
        grid=(n // tn, m // tm),
        in_specs=in_specs,
        out_specs=pl.BlockSpec((tm, tn), lambda j, i: (i, j)),
        out_shape=jax.ShapeDtypeStruct((m, n), out_dtype),
        scratch_shapes=[pltpu.VMEM((k, tn), BF16)],
        compiler_params=_cparams("arbitrary", "arbitrary"),
        name=name,
    )(*args)


def _gateup_kernel(x_ref, wg_ref, wu_ref, o_ref, wgb_ref, wub_ref):
    @pl.when(pl.program_id(1) == 0)
    def _():
        wgb_ref[...] = wg_ref[...].astype(BF16)
        wub_ref[...] = wu_ref[...].astype(BF16)

    x = x_ref[...]
    g = _dot(x, wgb_ref[...])
    u = _dot(x, wub_ref[...])
    o_ref[...] = (_silu(g) * u).astype(o_ref.dtype)


def gate_up(x, wg3, wu3, layer):
    m, k = x.shape
    f = wg3.shape[2]
    tm, tn = _matmul_tiles(m, k, f, 2, 2, False)
    wspec = pl.BlockSpec((None, k, tn), lambda j, i: (layer, 0, j))
    return pl.pallas_call(
        _gateup_kernel,
        grid=(f // tn, m // tm),
        in_specs=[pl.BlockSpec((tm, k), lambda j, i: (i, 0)), wspec, wspec],
        out_specs=pl.BlockSpec((tm, tn), lambda j, i: (i, j)),
        out_shape=jax.ShapeDtypeStruct((m, f), BF16),
        scratch_shapes=[pltpu.VMEM((k, tn), BF16), pltpu.VMEM((k, tn), BF16)],
        compiler_params=_cparams("arbitrary", "arbitrary"),
        name="gate_up",
    )(x, wg3, wu3)


def _retention_kernel(q_ref, k_ref, v_ref, g_ref, cos_ref, sin_ref, dec_ref, qd_ref, kd_ref,
                      cd_ref, s0_ref, y_ref, sout_ref, state_ref, *, n_chunks, dk, dv, hg):
    c = pl.program_id(2)

    @pl.when(c == 0)
    def _():
        state_ref[...] = s0_ref[0]

    half = dk // 2
    cos = cos_ref[...]
    sin = sin_ref[...]

    def rot(x):
        x1, x2 = x[:, :half], x[:, half:]
        return jnp.concatenate([x1 * cos - x2 * sin, x1 * sin + x2 * cos], axis=-1)

    for i in range(hg):
        q = rot(q_ref[:, i * dk:(i + 1) * dk])
        k = rot(k_ref[:, i * dk:(i + 1) * dk]) * (dk ** -0.5)
        qb = q.astype(BF16)
        kb = k.astype(BF16)
        vb = v_ref[:, i * dv:(i + 1) * dv]
        scores = _dot_nt(qb, kb) * dec_ref[i]
        inner = _dot(scores.astype(BF16), vb)
        st = state_ref[i]
        cross = _dot(qb, st.astype(BF16)) * qd_ref[i]
        kdt = jnp.transpose(k * kd_ref[i]).astype(BF16)
        state_ref[i] = st * cd_ref[i] + _dot(kdt, vb)
        y = inner + cross
        mu = jnp.mean(y, axis=-1, keepdims=True)
        d = y - mu
        var = jnp.mean(d * d, axis=-1, keepdims=True)
        yn = d * lax.rsqrt(var + GN_EPS)
        g = g_ref[:, i * dv:(i + 1) * dv].astype(F32)
        y_ref[:, i * dv:(i + 1) * dv] = (_silu(g) * yn).astype(y_ref.dtype)

    @pl.when(c == n_chunks - 1)
    def _():
        sout_ref[0] = state_ref[...]


def _retention_tables(heads, chunk, valid):
    lg = jnp.log1p(-jnp.exp2(-5.0 - jnp.arange(heads, dtype=F32)))
    i = jnp.arange(chunk, dtype=F32)
    ok = i < valid
    diff = i[:, None] - i[None, :]
    dec = jnp.where((diff[None] >= 0) & ok[None, :, None] & ok[None, None, :],
                    jnp.exp(jnp.maximum(diff, 0.0)[None] * lg[:, None, None]), 0.0)
    qd = jnp.where(ok[None, :], jnp.exp((i + 1.0)[None, :] * lg[:, None]), 0.0)
    kd = jnp.where(ok[None, :], jnp.exp((valid - 1.0 - i)[None, :] * lg[:, None]), 0.0)
    cd = jnp.exp(valid * lg)
    return dec, qd[:, :, None], kd[:, :, None], cd[:, None, None]


def _rope_tables(pos, half):
    inv = ROPE_BASE ** (-jnp.arange(half, dtype=F32) / half)
    ang = pos.astype(F32)[:, None] * inv[None, :]
    return jnp.cos(ang), jnp.sin(ang)


def retention(qk, vg, state0, pos, *, batch, n_chunks, chunk, valid, heads, dk, dv):
    rows = batch * n_chunks * chunk
    assert qk.shape[0] >= rows and vg.shape[0] >= rows
    hg = 2 if heads % 2 == 0 else 1
    ng = heads // hg
    dec, qd, kd, cd = _retention_tables(heads, chunk, valid)
    cos, sin = _rope_tables(pos, dk // 2)
    row = lambda b, h, c: b * n_chunks + c
    in_specs = [
        pl.BlockSpec((chunk, hg * dk), lambda b, h, c: (row(b, h, c), h)),
        pl.BlockSpec((chunk, hg * dk), lambda b, h, c: (row(b, h, c), ng + h)),
        pl.BlockSpec((chunk, hg * dv), lambda b, h, c: (row(b, h, c), h)),
        pl.BlockSpec((chunk, hg * dv), lambda b, h, c: (row(b, h, c), ng + h)),
        pl.BlockSpec((chunk, dk // 2), lambda b, h, c: (c, 0)),
        pl.BlockSpec((chunk, dk // 2), lambda b, h, c: (c, 0)),
        pl.BlockSpec((hg, chunk, chunk), lambda b, h, c: (h, 0, 0)),
        pl.BlockSpec((hg, chunk, 1), lambda b, h, c: (h, 0, 0)),
        pl.BlockSpec((hg, chunk, 1), lambda b, h, c: (h, 0, 0)),
        pl.BlockSpec((hg, 1, 1), lambda b, h, c: (h, 0, 0)),
        pl.BlockSpec((1, hg, dk, dv), lambda b, h, c: (b, h, 0, 0)),
    ]
    out_specs = [
        pl.BlockSpec((chunk, hg * dv), lambda b, h, c: (row(b, h, c), h)),
        pl.BlockSpec((1, hg, dk, dv), lambda b, h, c: (b, h, 0, 0)),
    ]
    return pl.pallas_call(
        functools.partial(_retention_kernel, n_chunks=n_chunks, dk=dk, dv=dv, hg=hg),
        grid=(batch, ng, n_chunks),
        in_specs=in_specs,
        out_specs=out_specs,
        out_shape=[jax.ShapeDtypeStruct((rows, heads * dv), BF16),
                   jax.ShapeDtypeStruct((batch, heads, dk, dv), F32)],
        scratch_shapes=[pltpu.VMEM((hg, dk, dv), F32)],
        compiler_params=_cparams("arbitrary", "arbitrary", "arbitrary"),
        name="retention",
    )(qk, qk, vg, vg, cos, sin, dec, qd, kd, cd, state0)


def _top_rows(gate_t, n_valid, n_sel):
    n, r = gate_t.shape
    row = lax.broadcasted_iota(jnp.int32, (n, r), 0)
    valid = row < n_valid
    g = jnp.where(valid, gate_t, -jnp.inf)
    rank = jnp.zeros((n, r), F32)
    for j in range(n):
        gj = g[j:j + 1, :]
        tie = jnp.where(gj == g, jnp.where(row > j, 1.0, 0.0), 0.0)
        rank = rank + jnp.where(gj > g, 1.0, tie)
    return jnp.where(valid, jnp.where(rank < n_sel, 1.0, 0.0), 0.0)


def _moba_prompt_kernel(q_ref, k_ref, v_ref, o_ref, kb_ref, vt_ref, means_ref, sel_ref, m_ref, l_ref,
                        acc_ref, *, n_blocks, hd, hg):
    t = pl.program_id(2)
    blk = MOBA_BLOCK
    scale = hd ** -0.5

    @pl.when(t == 0)
    def _():
        kb_ref[...] = k_ref[...].astype(BF16)
        for n in range(n_blocks):
            rows = slice(n * blk, (n + 1) * blk)
            means_ref[n:n + 1, :] = jnp.mean(k_ref[rows, :], axis=0, keepdims=True)
            for i in range(hg):
                vt_ref[i * hd:(i + 1) * hd, rows] = jnp.transpose(v_ref[rows, i * hd:(i + 1) * hd]).astype(BF16)

    r0 = pl.multiple_of(t * blk, blk)
    ki = lax.broadcasted_iota(jnp.int32, (blk, blk), 0)
    qi = lax.broadcasted_iota(jnp.int32, (blk, blk), 1)

    for i in range(hg):
        cs = slice(i * hd, (i + 1) * hd)
        q = q_ref[:, cs]
        gate_t = _dot_nt(means_ref[:, cs], q, precision=lax.Precision.HIGHEST)
        sel_ref[i] = _top_rows(gate_t, t, min(MOBA_TOPK, n_blocks))
        s = _dot_nt(kb_ref[pl.ds(r0, blk), cs], q.astype(BF16)) * scale
        s = jnp.where(ki <= qi, s, NEG_BIG)
        m0 = jnp.max(s, axis=0, keepdims=True)
        p = jnp.exp(s - m0)
        m_ref[i] = m0
        l_ref[i] = jnp.sum(p, axis=0, keepdims=True)
        acc_ref[i] = _dot(vt_ref[cs, pl.ds(r0, blk)], p.astype(BF16))

    def body(n, c):
        rn = pl.multiple_of(n * blk, blk)
        upd = []
        for i in range(hg):
            cs = slice(i * hd, (i + 1) * hd)
            s = _dot_nt(kb_ref[pl.ds(rn, blk), cs], q_ref[:, cs].astype(BF16)) * scale
            s = jnp.where(sel_ref[i, pl.ds(n, 1), :] > 0.5, s, NEG_BIG)
            m_new = jnp.maximum(m_ref[i], jnp.max(s, axis=0, keepdims=True))
            alpha = jnp.exp(m_ref[i] - m_new)
            p = jnp.exp(s - m_new)
            pv = _dot(vt_ref[cs, pl.ds(rn, blk)], p.astype(BF16))
            upd.append((m_new, alpha, jnp.sum(p, axis=0, keepdims=True), pv))
        for i, (m_new, alpha, psum, pv) in enumerate(upd):
            m_ref[i] = m_new
            l_ref[i] = alpha * l_ref[i] + psum
            acc_ref[i] = alpha * acc_ref[i] + pv
        return c

    lax.fori_loop(0, t, body, 0)
    for i in range(hg):
        o_ref[:, i * hd:(i + 1) * hd] = jnp.transpose(acc_ref[i] / l_ref[i]).astype(o_ref.dtype)


def moba_prompt(q, k, v, *, batch, seq, heads, hd):
    n_blocks = seq // MOBA_BLOCK
    assert seq % MOBA_BLOCK == 0
    hg = 4 if heads % 4 == 0 else 1
    qspec = pl.BlockSpec((MOBA_BLOCK, hg * hd), lambda b, h, t: (b * n_blocks + t, h))
    kvspec = pl.BlockSpec((seq, hg * hd), lambda b, h, t: (b, h))
    return pl.pallas_call(
        functools.partial(_moba_prompt_kernel, n_blocks=n_blocks, hd=hd, hg=hg),
        grid=(batch, heads // hg, n_blocks),
        in_specs=[qspec, kvspec, kvspec],
        out_specs=qspec,
        out_shape=jax.ShapeDtypeStruct((batch * seq, heads * hd), BF16),
        scratch_shapes=[pltpu.VMEM((seq, hg * hd), BF16), pltpu.VMEM((hg * hd, seq), BF16),
                        pltpu.VMEM((n_blocks, hg * hd), F32),
                        pltpu.VMEM((hg, n_blocks, MOBA_BLOCK), F32),
                        pltpu.VMEM((hg, 1, MOBA_BLOCK), F32), pltpu.VMEM((hg, 1, MOBA_BLOCK), F32),
                        pltpu.VMEM((hg, hd, MOBA_BLOCK), F32)],
        compiler_params=_cparams("arbitrary", "arbitrary", "arbitrary"),
        name="moba_prompt",
    )(q, k, v)


SAMPLE_ROWS = 8
MEANS_PAGES = 4


def _page_means_kernel(pt_ref, *refs, pages_per_block):
    kp = refs[:MEANS_PAGES]
    o_ref = refs[MEANS_PAGES]
    g = pl.program_id(1)
    for i in range(MEANS_PAGES // pages_per_block):
        tot = jnp.sum(kp[i * pages_per_block][...], axis=0)
        for j in range(1, pages_per_block):
            tot = tot + jnp.sum(kp[i * pages_per_block + j][...], axis=0)
        o_ref[0, g * (MEANS_PAGES // pages_per_block) + i] = tot * (1.0 / MOBA_BLOCK)


def page_block_means(cache_k, pt_flat, *, batch, n_pages):
    _, _, page, heads, hd = cache_k.shape
    ppb = MOBA_BLOCK // page
    nb = n_pages // ppb
    assert MEANS_PAGES % ppb == 0 and n_pages % MEANS_PAGES == 0

    def pspec(i):
        return pl.BlockSpec((None, None, page, heads, hd),
                            lambda b, g, pt: (0, pt[b * n_pages + g * MEANS_PAGES + i], 0, 0, 0))

    return pl.pallas_call(
        functools.partial(_page_means_kernel, pages_per_block=ppb),
        grid_spec=pltpu.PrefetchScalarGridSpec(
            num_scalar_prefetch=1,
            grid=(batch, n_pages // MEANS_PAGES),
            in_specs=[pspec(i) for i in range(MEANS_PAGES)],
            out_specs=pl.BlockSpec((1, nb, heads, hd), lambda b, g, pt: (b, 0, 0, 0)),
        ),
        out_shape=jax.ShapeDtypeStruct((batch, nb, heads, hd), F32),
        compiler_params=_cparams("arbitrary", "arbitrary"),
        name="page_means",
    )(pt_flat, *([cache_k] * MEANS_PAGES))


def _sample_gate_kernel(q_ref, m_ref, idx_ref, *, n_tok, n_sel):
    means = m_ref[0]
    nb = means.shape[0]
    blk_id = lax.broadcasted_iota(jnp.int32, (nb,) + means.shape[1:2] + (1,), 0)
    for t in range(n_tok):
        g = jnp.sum(means * q_ref[0, t][None], axis=-1, keepdims=True)
        for s in range(n_sel):
            m = jnp.max(g, axis=0, keepdims=True)
            idx = jnp.min(jnp.where(g == m, blk_id, nb), axis=0, keepdims=True)
            idx_ref[0, t * n_sel + s] = idx[0]
            g = jnp.where(blk_id == idx, -jnp.inf, g)


def sample_gate(q4, means, *, n_tok):
    batch, nb, heads, hd = means.shape
    assert nb >= MOBA_TOPK
    return pl.pallas_call(
        functools.partial(_sample_gate_kernel, n_tok=n_tok, n_sel=MOBA_TOPK),
        grid=(batch,),
        in_specs=[pl.BlockSpec((1, n_tok, heads, hd), lambda b: (b, 0, 0, 0)),
                  pl.BlockSpec((1, nb, heads, hd), lambda b: (b, 0, 0, 0))],
        out_specs=pl.BlockSpec((1, n_tok * MOBA_TOPK, heads, 1), lambda b: (b, 0, 0, 0)),
        out_shape=jax.ShapeDtypeStruct((batch, n_tok * MOBA_TOPK, heads, 1), jnp.int32),
        compiler_params=_cparams("arbitrary"),
        name="sample_gate",
    )(q4, means)


def _sample_attn_kernel(pp_ref, q_ref, kn_ref, vn_ref, ck_hbm, cv_hbm, o_ref, kbuf, vbuf, sem,
                        *, n_steps, heads, n_tok, n_pg, hd):
    step = pl.program_id(0)
    per_step = n_tok * n_pg

    def copies(st, slot):
        h = st % heads
        out = []
        for i in range(per_step):
            pg = pp_ref[st * per_step + i]
            out.append(pltpu.make_async_copy(ck_hbm.at[0, pg, :, h, :], kbuf.at[slot, i], sem.at[0, slot]))
            out.append(pltpu.make_async_copy(cv_hbm.at[0, pg, :, h, :], vbuf.at[slot, i], sem.at[1, slot]))
        return out

    slot = step % 2

    @pl.when(step == 0)
    def _():
        for cp in copies(step, 0):
            cp.start()

    @pl.when(step + 1 < n_steps)
    def _():
        for cp in copies(step + 1, 1 - slot):
            cp.start()

    for cp in copies(step, slot):
        cp.wait()

    r = SAMPLE_ROWS
    scale = hd ** -0.5
    qb = q_ref[...].astype(BF16)
    knb = kn_ref[...].astype(BF16)
    vnb = vn_ref[...].astype(BF16)
    qi = lax.broadcasted_iota(jnp.int32, (r, r), 0)
    ki = lax.broadcasted_iota(jnp.int32, (r, r), 1)
    s_own = _dot_nt(qb, knb) * scale
    s_own = jnp.where((ki <= qi) & (ki < n_tok), s_own, NEG_BIG)
    m_own = jnp.max(s_own, axis=-1, keepdims=True)
    o_ref[...] = jnp.zeros_like(o_ref)
    for t in range(n_tok):
        s_pg = [_dot_nt(qb, kbuf[slot, t * n_pg + j].astype(BF16)) * scale for j in range(n_pg)]
        m = m_own
        for s in s_pg:
            m = jnp.maximum(m, jnp.max(s, axis=-1, keepdims=True))
        p_own = jnp.exp(s_own - m)
        l = jnp.sum(p_own, axis=-1, keepdims=True)
        acc = _dot(p_own.astype(BF16), vnb)
        for j in range(n_pg):
            p = jnp.exp(s_pg[j] - m)
            l = l + jnp.sum(p, axis=-1, keepdims=True)
            acc = acc + _dot(p.astype(BF16), vbuf[slot, t * n_pg + j].astype(BF16))
        out = acc / l
        o_ref[t:t + 1, :] = out[t:t + 1, :]


def sample_attention(q8, kn8, vn8, cache_k, cache_v, pp_flat, *, batch, heads, hd, n_tok, page):
    ppb = MOBA_BLOCK // page
    n_pg = MOBA_TOPK * ppb
    n_steps = batch * heads
    small = pl.BlockSpec((SAMPLE_ROWS, hd), lambda s, pp: (s // heads, s % heads))
    return pl.pallas_call(
        functools.partial(_sample_attn_kernel, n_steps=n_steps, heads=heads, n_tok=n_tok, n_pg=n_pg, hd=hd),
        grid_spec=pltpu.PrefetchScalarGridSpec(
            num_scalar_prefetch=1,
            grid=(n_steps,),
            in_specs=[small, small, small, pl.BlockSpec(memory_space=pl.ANY), pl.BlockSpec(memory_space=pl.ANY)],
            out_specs=small,
            scratch_shapes=[pltpu.VMEM((2, n_tok * n_pg, page, hd), F32),
                            pltpu.VMEM((2, n_tok * n_pg, page, hd), F32),
                            pltpu.SemaphoreType.DMA((2, 2))],
        ),
        out_shape=jax.ShapeDtypeStruct((batch * SAMPLE_ROWS, heads * hd), F32),
        compiler_params=_cparams("arbitrary"),
        name="sample_attn",
    )(pp_flat, q8, kn8, vn8, cache_k, cache_v)


def _norm_router_kernel(x_ref, w_ref, wr_ref, h_ref, info_ref, *, n_experts):
    x = x_ref[...]
    y = x * lax.rsqrt(jnp.mean(x * x, axis=-1, keepdims=True) + RMS_EPS) * w_ref[...]
    h_ref[...] = y
    logits = jnp.dot(y, wr_ref[...], precision=lax.Precision.HIGHEST, preferred_element_type=F32)
    r, n = logits.shape
    col = lax.broadcasted_iota(jnp.int32, (r, n), 1)
    lg = jnp.where(col < n_experts, logits, -jnp.inf)
    m1 = jnp.max(lg, axis=-1, keepdims=True)
    i1 = jnp.min(jnp.where(lg == m1, col, n), axis=-1, keepdims=True)
    lg2 = jnp.where(col == i1, -jnp.inf, lg)
    m2 = jnp.max(lg2, axis=-1, keepdims=True)
    i2 = jnp.min(jnp.where(lg2 == m2, col, n), axis=-1, keepdims=True)
    e2 = jnp.exp(m2 - m1)
    den = 1.0 + e2
    w1 = 1.0 / den
    w2 = e2 / den
    info = jnp.where(col == 0, i1.astype(F32),
                     jnp.where(col == 1, i2.astype(F32),
                               jnp.where(col == 2, w1, jnp.where(col == 3, w2, 0.0))))
    info_ref[...] = info


def norm_router(x, w, w_router):
    m, d = x.shape
    e = w_router.shape[1]
    tm = _pick(m, (256, 128, 64, 32, 16, 8))
    wr = jnp.zeros((d, 128), F32).at[:, :e].set(w_router)
    return pl.pallas_call(
        functools.partial(_norm_router_kernel, n_experts=e),
        grid=(m // tm,),
        in_specs=[pl.BlockSpec((tm, d), lambda i: (i, 0)), pl.BlockSpec((1, d), lambda i: (0, 0)),
                  pl.BlockSpec((d, 128), lambda i: (0, 0))],
        out_specs=[pl.BlockSpec((tm, d), lambda i: (i, 0)), pl.BlockSpec((tm, 128), lambda i: (i, 0))],
        out_shape=[jax.ShapeDtypeStruct((m, d), F32), jax.ShapeDtypeStruct((m, 128), F32)],
        compiler_params=_cparams("parallel"),
        name="norm_router",
    )(x, w.reshape(1, d), wr)


GATHER_TILE = 256


def _gather_rows(src_hbm, dst_ref, sem, row_of, n_rows):
    assert dst_ref.shape[0] == n_rows

    def start(r, c):
        pltpu.make_async_copy(src_hbm.at[pl.ds(row_of(r), 1), :], dst_ref.at[pl.ds(r, 1), :], sem).start()
        return c

    lax.fori_loop(0, n_rows, start, 0, unroll=8)
    pltpu.make_async_copy(dst_ref, dst_ref, sem).wait()


def _gather_cast_kernel(src_ref, ntile_ref, h_hbm, o_ref, buf_ref, sem):
    i = pl.program_id(0)

    @pl.when(i < ntile_ref[0])
    def _():
        base = i * GATHER_TILE
        _gather_rows(h_hbm, buf_ref, sem, lambda r: src_ref[base + r], GATHER_TILE)
        o_ref[...] = buf_ref[...].astype(o_ref.dtype)

    @pl.when(i >= ntile_ref[0])
    def _():
        o_ref[...] = jnp.zeros_like(o_ref)


def gather_cast(h, src_rows, n_tiles_used):
    s_rows = src_rows.shape[0]
    d = h.shape[1]
    return pl.pallas_call(
        _gather_cast_kernel,
        grid_spec=pltpu.PrefetchScalarGridSpec(
            num_scalar_prefetch=2,
            grid=(s_rows // GATHER_TILE,),
            in_specs=[pl.BlockSpec(memory_space=pl.ANY)],
            out_specs=pl.BlockSpec((GATHER_TILE, d), lambda i, s, n: (i, 0)),
            scratch_shapes=[pltpu.VMEM((GATHER_TILE, d), F32), pltpu.SemaphoreType.DMA(())],
        ),
        out_shape=jax.ShapeDtypeStruct((s_rows, d), BF16),
        compiler_params=_cparams("arbitrary"),
        name="gather_cast",
    )(src_rows, n_tiles_used, h)


EXPERT_SUB = 256
SEGMENT_ROWS = 2304


def _experts_kernel(seg_e_ref, seg_rows_ref, nseg_ref, x_ref, wg_ref, wu_ref, wd_ref, o_ref,
                    wgb_ref, wub_ref, wdb_ref):
    s = pl.program_id(0)
    f = pl.program_id(1)

    @pl.when(f == 0)
    def _():
        o_ref[...] = jnp.zeros_like(o_ref)

    def tile(r0, size):
        xs = x_ref[pl.ds(r0, size), :]
        g = _dot(xs, wgb_ref[...])
        u = _dot(xs, wub_ref[...])
        a = (_silu(g) * u).astype(BF16)
        o_ref[pl.ds(r0, size), :] += _dot(a, wdb_ref[...])

    @pl.when(s < nseg_ref[0])
    def _():
        wgb_ref[...] = wg_ref[...].astype(BF16)
        wub_ref[...] = wu_ref[...].astype(BF16)
        wdb_ref[...] = wd_ref[...].astype(BF16)
        rows = seg_rows_ref[s]
        big = 2 * EXPERT_SUB
        n_big = (rows + EXPERT_SUB - 1) // big

        def body(i, c):
            tile(pl.multiple_of(i * big, big), big)
            return c

        lax.fori_loop(0, n_big, body, 0)

        @pl.when(n_big * big < rows)
        def _():
            tile(pl.multiple_of(n_big * big, big), EXPERT_SUB)


def experts(x_sorted, wg, wu, wd, layer, seg_e, seg_rows, nseg, *, seg_cap):
    s_rows, d = x_sorted.shape
    s_max = s_rows // seg_cap
    f_dim = wg.shape[3]
    tf = _pick(f_dim, (256, 128))
    nf = f_dim // tf

    def seg_blk(s, n):
        return jnp.minimum(s, n[0] - 1)

    def fblk(s, f, n):
        return jnp.where(s < n[0], f, nf - 1)

    one = pl.Buffered(1)
    in_specs = [
        pl.BlockSpec((seg_cap, d), lambda s, f, e, r, n: (seg_blk(s, n), 0), pipeline_mode=one),
        pl.BlockSpec((None, None, d, tf), lambda s, f, e, r, n: (layer, e[s], 0, fblk(s, f, n))),
        pl.BlockSpec((None, None, d, tf), lambda s, f, e, r, n: (layer, e[s], 0, fblk(s, f, n))),
        pl.BlockSpec((None, None, tf, d), lambda s, f, e, r, n: (layer, e[s], fblk(s, f, n), 0)),
    ]
    return pl.pallas_call(
        _experts_kernel,
        grid_spec=pltpu.PrefetchScalarGridSpec(
            num_scalar_prefetch=3,
            grid=(s_max, nf),
            in_specs=in_specs,
            out_specs=pl.BlockSpec((seg_cap, d), lambda s, f, e, r, n: (s, 0), pipeline_mode=one),
            scratch_shapes=[pltpu.VMEM((d, tf), BF16), pltpu.VMEM((d, tf), BF16),
                            pltpu.VMEM((tf, d), BF16)],
        ),
        out_shape=jax.ShapeDtypeStruct((s_rows, d), F32),
        compiler_params=_cparams("arbitrary", "arbitrary"),
        name="experts",
    )(seg_e, seg_rows, nseg, x_sorted, wg, wu, wd)


def _combine_norm_kernel(s1_ref, s2_ref, ys_hbm, x_ref, info_ref, w_ref, o_ref, a_ref, b_ref, sem):
    i = pl.program_id(0)
    tm = x_ref.shape[0]
    base = i * tm
    _gather_rows(ys_hbm, a_ref, sem.at[0], lambda r: s1_ref[base + r], tm)
    _gather_rows(ys_hbm, b_ref, sem.at[1], lambda r: s2_ref[base + r], tm)
    info = info_ref[...]
    y = info[:, 2:3] * a_ref[...] + info[:, 3:4] * b_ref[...]
    x = x_ref[...] + y
    xn = x * lax.rsqrt(jnp.mean(x * x, axis=-1, keepdims=True) + RMS_EPS)
    o_ref[...] = xn * w_ref[...]


def combine_norm(x, y_sorted, slot1, slot2, info, w):
    m, d = x.shape
    tm = _pick(m, (256, 128, 64, 32, 16, 8))
    return pl.pallas_call(
        _combine_norm_kernel,
        grid_spec=pltpu.PrefetchScalarGridSpec(
            num_scalar_prefetch=2,
            grid=(m // tm,),
            in_specs=[pl.BlockSpec(memory_space=pl.ANY),
                      pl.BlockSpec((tm, d), lambda i, a, b: (i, 0)),
                      pl.BlockSpec((tm, 128), lambda i, a, b: (i, 0)),
                      pl.BlockSpec((1, d), lambda i, a, b: (0, 0))],
            out_specs=pl.BlockSpec((tm, d), lambda i, a, b: (i, 0)),
            scratch_shapes=[pltpu.VMEM((tm, d), F32), pltpu.VMEM((tm, d), F32),
                            pltpu.SemaphoreType.DMA((2,))],
        ),
        out_shape=jax.ShapeDtypeStruct((m, d), F32),
        compiler_params=_cparams("arbitrary"),
        name="combine_norm",
    )(slot1, slot2, y_sorted, x, info, w.reshape(1, d))


def _route_tables(info, n_real, n_experts, seg_cap):
    m = info.shape[0]
    e = info[:, :MOE_TOPK].astype(jnp.int32)
    valid = (jnp.arange(m) < n_real)[:, None]
    onehot = ((e[:, :, None] == jnp.arange(n_experts)[None, None, :]) & valid[:, :, None])
    onehot = onehot.reshape(m * MOE_TOPK, n_experts).astype(jnp.int32)
    rank = jnp.sum((jnp.cumsum(onehot, axis=0) - onehot) * onehot, axis=1)
    counts = jnp.sum(onehot, axis=0)
    nseg_e = (counts + seg_cap - 1) // seg_cap
    seg_start = jnp.cumsum(nseg_e) - nseg_e
    nseg = jnp.sum(nseg_e)
    s_max = (m * MOE_TOPK) // seg_cap + n_experts
    flat_e = e.reshape(-1)
    flat_valid = jnp.broadcast_to(valid, (m, MOE_TOPK)).reshape(-1)
    slot = (seg_start[flat_e] + rank // seg_cap) * seg_cap + rank % seg_cap
    slot = jnp.where(flat_valid, slot, 0)
    sid = jnp.arange(s_max)
    owner = jnp.argmax((sid[:, None] >= seg_start[None, :]) & (sid[:, None] < (seg_start + nseg_e)[None, :]), axis=1)
    seg_e = jnp.where(sid < nseg, owner, owner[jnp.maximum(nseg - 1, 0)]).astype(jnp.int32)
    within = sid - seg_start[seg_e]
    seg_rows = jnp.clip(counts[seg_e] - within * seg_cap, 0, seg_cap)
    seg_rows = jnp.where(sid < nseg, seg_rows, 0).astype(jnp.int32)
    tok = jnp.repeat(jnp.arange(m, dtype=jnp.int32), MOE_TOPK)
    src = jnp.full((s_max * seg_cap,), m - 1, jnp.int32)
    src = src.at[jnp.where(flat_valid, slot, s_max * seg_cap)].set(tok, mode="drop")
    n_tiles_used = (nseg * (seg_cap // GATHER_TILE)).astype(jnp.int32).reshape(1)
    slots = slot.reshape(m, MOE_TOPK).astype(jnp.int32)
    return src, n_tiles_used, seg_e, seg_rows, nseg.astype(jnp.int32).reshape(1), slots[:, 0], slots[:, 1]


def kernel(x_prompt, x_sample, state_ret, cache_k, cache_v, page_table, norm_w, final_norm_w,
           ret_w_in, ret_w_out, moba_w_qkv, moba_w_out, ffn_w_gate, ffn_w_up, ffn_w_down,
           moe_w_router, moe_w_gate, moe_w_up, moe_w_down):
    bp, sp, d = x_prompt.shape
    bs, ts, _ = x_sample.shape
    _, _, r_heads, dk, dv = state_ret.shape
    _, n_pool, page, m_heads, hd = cache_k.shape
    n_pages = page_table.shape[1]
    past_len = n_pages * page
    n_experts = moe_w_router.shape[2]
    assert sp % RET_CHUNK == 0 and ts < RET_CHUNK and ts <= SAMPLE_ROWS
    assert MOBA_BLOCK % page == 0 and past_len % MOBA_BLOCK == 0 and sp % MOBA_BLOCK == 0

    n_p = bp * sp
    n_s = bs * ts
    n_real = n_p + n_s
    m = -(-(n_real + 1) // ROW_ALIGN) * ROW_ALIGN
    x = jnp.concatenate([x_prompt.reshape(n_p, d), x_sample.reshape(n_s, d),
                         jnp.zeros((m - n_real, d), F32)], axis=0)

    h = rmsnorm(x, norm_w[0, 0], BF16)
    n_qk = 2 * r_heads * dk
    qk = matmul(h, ret_w_in, 0, n_off=0, n=n_qk, name="ret_qk_proj")
    vg = matmul(h, ret_w_in, 0, n_off=n_qk, n=2 * r_heads * dv, out_dtype=BF16, name="ret_vg_proj")
    pos_p = jnp.arange(sp, dtype=jnp.int32)
    zero_state = jnp.zeros((bp, r_heads, dk, dv), F32)
    y_p, st_p = retention(qk, vg, zero_state, pos_p, batch=bp, n_chunks=sp // RET_CHUNK,
                          chunk=RET_CHUNK, valid=RET_CHUNK, heads=r_heads, dk=dk, dv=dv)

    def pad_chunk(a2):
        a3 = jnp.pad(a2[n_p:n_real].reshape(bs, ts, -1), ((0, 0), (0, RET_CHUNK - ts), (0, 0)))
        return a3.reshape(bs * RET_CHUNK, -1)

    pos_s = past_len + jnp.arange(RET_CHUNK, dtype=jnp.int32)
    y_s, st_s = retention(pad_chunk(qk), pad_chunk(vg), state_ret[0], pos_s, batch=bs, n_chunks=1,
                          chunk=RET_CHUNK, valid=ts, heads=r_heads, dk=dk, dv=dv)
    y_s = y_s.reshape(bs, RET_CHUNK, -1)[:, :ts].reshape(n_s, -1)
    y = jnp.concatenate([y_p[:n_p], y_s, jnp.zeros((m - n_real, y_s.shape[1]), BF16)], axis=0)
    x = matmul(y, ret_w_out, 0, res=x, name="ret_out_proj")

    h = rmsnorm(x, norm_w[0, 1], BF16)
    a = gate_up(h, ffn_w_gate, ffn_w_up, 0)
    x = matmul(a, ffn_w_down, 0, res=x, name="ffn_down")

    h = rmsnorm(x, norm_w[1, 0], BF16)
    q = matmul(h, moba_w_qkv, 0, n_off=0, n=d, name="moba_q_proj")
    k = matmul(h, moba_w_qkv, 0, n_off=d, n=d, name="moba_k_proj")
    v = matmul(h, moba_w_qkv, 0, n_off=2 * d, n=d, name="moba_v_proj")
    o_p = moba_prompt(q, k, v, batch=bp, seq=sp, heads=m_heads, hd=hd)

    def pad8(a2):
        return jnp.pad(a2[n_p:n_real].reshape(bs, ts, d),
                       ((0, 0), (0, SAMPLE_ROWS - ts), (0, 0))).reshape(bs * SAMPLE_ROWS, d)

    q8, k8, v8 = pad8(q), pad8(k), pad8(v)
    means = page_block_means(cache_k, page_table.reshape(-1), batch=bs, n_pages=n_pages)
    sel = sample_gate(q[n_p:n_real].reshape(bs, ts, m_heads, hd), means, n_tok=ts)
    sel = sel.reshape(bs, ts, MOBA_TOPK, m_heads).transpose(0, 3, 1, 2)
    ppb = MOBA_BLOCK // page
    lpage = sel[..., None] * ppb + jnp.arange(ppb, dtype=jnp.int32)
    pp = page_table[jnp.arange(bs)[:, None, None, None], lpage.reshape(bs, m_heads, ts, MOBA_TOPK * ppb)]
    o_s8 = sample_attention(q8, k8, v8, cache_k, cache_v, pp.reshape(-1).astype(jnp.int32),
                            batch=bs, heads=m_heads, hd=hd, n_tok=ts, page=page)
    o_s = o_s8.reshape(bs, SAMPLE_ROWS, d)[:, :ts].reshape(n_s, d).astype(BF16)
    o = jnp.concatenate([o_p, o_s, jnp.zeros((m - n_real, d), BF16)], axis=0)
    x = matmul(o, moba_w_out, 0, res=x, name="moba_out_proj")

    hf, info = norm_router(x, norm_w[1, 1], moe_w_router[0])
    seg_cap = SEGMENT_ROWS if m * MOE_TOPK >= 4 * SEGMENT_ROWS else 2 * GATHER_TILE
    src, n_tiles_used, seg_e, seg_rows, nseg, slot1, slot2 = _route_tables(info, n_real, n_experts, seg_cap)
    x_sorted = gather_cast(hf, src, n_tiles_used)
    y_sorted = experts(x_sorted, moe_w_gate, moe_w_up, moe_w_down, 0, seg_e, seg_rows, nseg, seg_cap=seg_cap)
    y = combine_norm(x, y_sorted, slot1, slot2, info, final_norm_w)

    y_prompt = y[:n_p].reshape(bp, sp, d)
    y_sample = y[n_p:n_real].reshape(bs, ts, d)
    kv_shape_p = (1, bp, sp, m_heads, hd)
    kv_shape_s = (1, bs, ts, m_heads, hd)
    return (y_prompt, y_sample, st_p[None], st_s[None],
            k[:n_p].reshape(kv_shape_p), v[:n_p].reshape(kv_shape_p),
            k[n_p:n_real].reshape(kv_shape_s), v[n_p:n_real].reshape(kv_shape_s))
```

```python
import functools

import jax
import jax.numpy as jnp
from jax import lax
from jax.experimental import pallas as pl
from jax.experimental.pallas import tpu as pltpu

RET_CHUNK = 128
ROPE_BASE = 10000.0
MOBA_BLOCK = 256
MOBA_TOPK = 3
MOE_TOPK = 2
RMS_EPS = 1e-6
GN_EPS = 1e-5

V7X_VMEM_LIMIT_BYTES = 56 * 1024 * 1024
ROW_ALIGN = 256
NEG_BIG = -1e30

F32 = jnp.float32
BF16 = jnp.bfloat16


def _cparams(*sem):
    return pltpu.CompilerParams(dimension_semantics=sem, vmem_limit_bytes=V7X_VMEM_LIMIT_BYTES)


def _pick(n, prefs):
    for p in prefs:
        if p <= n and n % p == 0:
            return p
    return n


def _silu(g):
    return g * (1.0 / (1.0 + jnp.exp(-g)))


def _dot(a, b):
    return jnp.dot(a, b, preferred_element_type=F32)


def _dot_nt(a, b, precision=None):
    return lax.dot_general(a, b, (((1,), (1,)), ((), ())), precision=precision,
                           preferred_element_type=F32)


def _split(x):
    hi = x.astype(BF16)
    return hi, (x - hi.astype(F32)).astype(BF16)


def _dot3(a, b):
    ah, al = _split(a)
    bh, bl = _split(b)
    return _dot(ah, bh) + (_dot(al, bh) + _dot(ah, bl))


def _dot3_nt(a, b):
    ah, al = _split(a)
    bh, bl = _split(b)
    return _dot_nt(ah, bh) + (_dot_nt(al, bh) + _dot_nt(ah, bl))


def _dot1(a, b):
    return _dot(a.astype(BF16), b.astype(BF16))


def _dot1_nt(a, b):
    return _dot_nt(a.astype(BF16), b.astype(BF16))


def _rmsnorm_kernel(x_ref, w_ref, o_ref):
    x = x_ref[...]
    y = x * lax.rsqrt(jnp.mean(x * x, axis=-1, keepdims=True) + RMS_EPS)
    o_ref[...] = (y * w_ref[...]).astype(o_ref.dtype)


def rmsnorm(x, w, out_dtype):
    m, d = x.shape
    tm = _pick(m, (256, 128, 64, 32, 16, 8))
    return pl.pallas_call(
        _rmsnorm_kernel,
        grid=(m // tm,),
        in_specs=[pl.BlockSpec((tm, d), lambda i: (i, 0)), pl.BlockSpec((1, d), lambda i: (0, 0))],
        out_specs=pl.BlockSpec((tm, d), lambda i: (i, 0)),
        out_shape=jax.ShapeDtypeStruct((m, d), out_dtype),
        compiler_params=_cparams("parallel"),
        name="rmsnorm",
    )(x, w.reshape(1, d))


def _mm_kernel(*refs, has_res):
    if has_res:
        x_ref, w_ref, r_ref, o_ref, wb_ref = refs
    else:
        x_ref, w_ref, o_ref, wb_ref = refs

    @pl.when(pl.program_id(1) == 0)
    def _():
        wb_ref[...] = w_ref[...].astype(BF16)

    acc = _dot(x_ref[...], wb_ref[...])
    if has_res:
        acc = acc + r_ref[...]
    o_ref[...] = acc.astype(o_ref.dtype)


TILE_VMEM_BUDGET = 44 * 1024 * 1024


def _matmul_tiles(m, k, n, n_w, out_bytes, has_res):
    best = None
    for tm in (2048, 1056, 1024, 528, 512, 256, 128, 64, 32, 16):
        if m % tm:
            continue
        for tn in (1024, 512, 256, 128):
            if n % tn:
                continue
            need = (2 * tm * k * 2 + n_w * (2 * k * tn * 4 + k * tn * 2)
                    + 2 * tm * tn * out_bytes + (2 * tm * tn * 4 if has_res else 0))
            score = (-(m // tm) * (n // tn), tn)
            if need <= TILE_VMEM_BUDGET and (best is None or score > best[0]):
                best = (score, (tm, tn))
    assert best is not None
    return best[1]


def matmul(x, w3, layer, *, name, n_off=0, n=None, res=None, out_dtype=F32):
    m, k = x.shape
    n = w3.shape[2] if n is None else n
    tm, tn = _matmul_tiles(m, k, n, 1, jnp.dtype(out_dtype).itemsize, res is not None)
    assert n_off % tn == 0
    off = n_off // tn
    in_specs = [pl.BlockSpec((tm, k), lambda j, i: (i, 0)),
                pl.BlockSpec((None, k, tn), lambda j, i: (layer, 0, j + off))]
    args = [x, w3]
    if res is not None:
        in_specs.append(pl.BlockSpec((tm, tn), lambda j, i: (i, j)))
        args.append(res)
    return pl.pallas_call(
        functools.partial(_mm_kernel, has_res=res is not None),
        grid=(n // tn, m // tm),
        in_specs=in_specs,
        out_specs=pl.BlockSpec((tm, tn), lambda j, i: (i, j)),
        out_shape=jax.ShapeDtypeStruct((m, n), out_dtype),
        scratch_shapes=[pltpu.VMEM((k, tn), BF16)],
        compiler_params=_cparams("arbitrary", "arbitrary"),
        name=name,
    )(*args)


def _gateup_kernel(x_ref, wg_ref, wu_ref, o_ref, wgb_ref, wub_ref):
    @pl.when(pl.program_id(1) == 0)
    def _():
        wgb_ref[...] = wg_ref[...].astype(BF16)
        wub_ref[...] = wu_ref[...].astype(BF16)

    x = x_ref[...]
    g = _dot(x, wgb_ref[...])
    u = _dot(x, wub_ref[...])
    o_ref[...] = (_silu(g) * u).astype(o_ref.dtype)


def gate_up(x, wg3, wu3, layer):
    m, k = x.shape
    f = wg3.shape[2]
    tm, tn = _matmul_tiles(m, k, f, 2, 2, False)
    wspec = pl.BlockSpec((None, k, tn), lambda j, i: (layer, 0, j))
    return pl.pallas_call(
        _gateup_kernel,
        grid=(f // tn, m // tm),
        in_specs=[pl.BlockSpec((tm, k), lambda j, i: (i, 0)), wspec, wspec],
        out_specs=pl.BlockSpec((tm, tn), lambda j, i: (i, j)),
        out_shape=jax.ShapeDtypeStruct((m, f), BF16),
        scratch_shapes=[pltpu.VMEM((k, tn), BF16), pltpu.VMEM((k, tn), BF16)],
        compiler_params=_cparams("arbitrary", "arbitrary"),
        name="gate_up",
    )(x, wg3, wu3)


PRECISE_W_TILE_BYTES = 8 * 1024 * 1024


def _precise_tn(k, n, n_w=1):
    fits = [tn for tn in range(128, n + 1, 128) if n % tn == 0 and n_w * k * tn * 4 <= PRECISE_W_TILE_BYTES]
    assert fits
    return fits[-1]


def _mm3_kernel(*refs, has_res):
    if has_res:
        x_ref, w_ref, r_ref, o_ref = refs
    else:
        x_ref, w_ref, o_ref = refs
    acc = _dot3(x_ref[...], w_ref[...])
    if has_res:
        acc = acc + r_ref[...]
    o_ref[...] = acc


def matmul_precise(x, w3, layer, *, name, n_off=0, n=None, res=None):
    m, k = x.shape
    n = w3.shape[2] if n is None else n
    tn = _precise_tn(k, n)
    assert n_off % tn == 0
    off = n_off // tn
    in_specs = [pl.BlockSpec((m, k), lambda j: (0, 0)),
                pl.BlockSpec((None, k, tn), lambda j: (layer, 0, j + off))]
    args = [x, w3]
    if res is not None:
        in_specs.append(pl.BlockSpec((m, tn), lambda j: (0, j)))
        args.append(res)
    return pl.pallas_call(
        functools.partial(_mm3_kernel, has_res=res is not None),
        grid=(n // tn,),
        in_specs=in_specs,
        out_specs=pl.BlockSpec((m, tn), lambda j: (0, j)),
        out_shape=jax.ShapeDtypeStruct((m, n), F32),
        compiler_params=_cparams("arbitrary"),
        name=name,
    )(*args)


def _gateup3_kernel(x_ref, wg_ref, wu_ref, o_ref):
    x = x_ref[...]
    g = _dot3(x, wg_ref[...])
    u = _dot3(x, wu_ref[...])
    o_ref[...] = _silu(g) * u


def gate_up_precise(x, wg3, wu3, layer):
    m, k = x.shape
    f = wg3.shape[2]
    tn = _precise_tn(k, f, 2)
    wspec = pl.BlockSpec((None, k, tn), lambda j: (layer, 0, j))
    return pl.pallas_call(
        _gateup3_kernel,
        grid=(f // tn,),
        in_specs=[pl.BlockSpec((m, k), lambda j: (0, 0)), wspec, wspec],
        out_specs=pl.BlockSpec((m, tn), lambda j: (0, j)),
        out_shape=jax.ShapeDtypeStruct((m, f), F32),
        compiler_params=_cparams("arbitrary"),
        name="gate_up_precise",
    )(x, wg3, wu3)


def _retention_kernel(q_ref, k_ref, v_ref, g_ref, cos_ref, sin_ref, dec_ref, qd_ref, kd_ref,
                      cd_ref, s0_ref, y_ref, sout_ref, state_ref, *, n_chunks, dk, dv, hg, precise):
    c = pl.program_id(2)

    @pl.when(c == 0)
    def _():
        state_ref[...] = s0_ref[0]

    half = dk // 2
    cos = cos_ref[...]
    sin = sin_ref[...]

    def rot(x):
        x1, x2 = x[:, :half], x[:, half:]
        return jnp.concatenate([x1 * cos - x2 * sin, x1 * sin + x2 * cos], axis=-1)

    qs, ks, vs, sts = [], [], [], []
    for i in range(hg):
        qs.append(rot(q_ref[:, i * dk:(i + 1) * dk]))
        ks.append(rot(k_ref[:, i * dk:(i + 1) * dk]) * (dk ** -0.5))
        vs.append(v_ref[:, i * dv:(i + 1) * dv])
        sts.append(state_ref[i])
    mm, mm_nt = (_dot3, _dot3_nt) if precise else (_dot1, _dot1_nt)
    raw = [mm_nt(qs[i], ks[i]) for i in range(hg)]
    crosses = [mm(qs[i], sts[i]) for i in range(hg)]
    for i in range(hg):
        kdt = jnp.transpose(ks[i] * kd_ref[i])
        state_ref[i] = sts[i] * cd_ref[i] + mm(kdt, vs[i])
    inners = [mm(raw[i] * dec_ref[i], vs[i]) for i in range(hg)]
    for i in range(hg):
        y = inners[i] + crosses[i] * qd_ref[i]
        mu = jnp.mean(y, axis=-1, keepdims=True)
        d = y - mu
        var = jnp.mean(d * d, axis=-1, keepdims=True)
        yn = d * lax.rsqrt(var + GN_EPS)
        g = g_ref[:, i * dv:(i + 1) * dv].astype(F32)
        y_ref[:, i * dv:(i + 1) * dv] = (_silu(g) * yn).astype(y_ref.dtype)

    @pl.when(c == n_chunks - 1)
    def _():
        sout_ref[0] = state_ref[...]


def _retention_tables(heads, chunk, valid):
    lg = jnp.log1p(-jnp.exp2(-5.0 - jnp.arange(heads, dtype=F32)))
    i = jnp.arange(chunk, dtype=F32)
    ok = i < valid
    diff = i[:, None] - i[None, :]
    dec = jnp.where((diff[None] >= 0) & ok[None, :, None] & ok[None, None, :],
                    jnp.exp(jnp.maximum(diff, 0.0)[None] * lg[:, None, None]), 0.0)
    qd = jnp.where(ok[None, :], jnp.exp((i + 1.0)[None, :] * lg[:, None]), 0.0)
    kd = jnp.where(ok[None, :], jnp.exp((valid - 1.0 - i)[None, :] * lg[:, None]), 0.0)
    cd = jnp.exp(valid * lg)
    return dec, qd[:, :, None], kd[:, :, None], cd[:, None, None]


def _rope_tables(pos, half):
    inv = ROPE_BASE ** (-jnp.arange(half, dtype=F32) / half)
    ang = pos.astype(F32)[:, None] * inv[None, :]
    return jnp.cos(ang), jnp.sin(ang)


def retention(qk, vg, state0, pos, *, batch, n_chunks, chunk, valid, heads, dk, dv, precise=False):
    rows = batch * n_chunks * chunk
    assert qk.shape[0] >= rows and vg.shape[0] >= rows
    hg = 2 if heads % 2 == 0 else 1
    ng = heads // hg
    dec, qd, kd, cd = _retention_tables(heads, chunk, valid)
    cos, sin = _rope_tables(pos, dk // 2)
    row = lambda b, h, c: b * n_chunks + c
    in_specs = [
        pl.BlockSpec((chunk, hg * dk), lambda b, h, c: (row(b, h, c), h)),
        pl.BlockSpec((chunk, hg * dk), lambda b, h, c: (row(b, h, c), ng + h)),
        pl.BlockSpec((chunk, hg * dv), lambda b, h, c: (row(b, h, c), h)),
        pl.BlockSpec((chunk, hg * dv), lambda b, h, c: (row(b, h, c), ng + h)),
        pl.BlockSpec((chunk, dk // 2), lambda b, h, c: (c, 0)),
        pl.BlockSpec((chunk, dk // 2), lambda b, h, c: (c, 0)),
        pl.BlockSpec((hg, chunk, chunk), lambda b, h, c: (h, 0, 0)),
        pl.BlockSpec((hg, chunk, 1), lambda b, h, c: (h, 0, 0)),
        pl.BlockSpec((hg, chunk, 1), lambda b, h, c: (h, 0, 0)),
        pl.BlockSpec((hg, 1, 1), lambda b, h, c: (h, 0, 0)),
        pl.BlockSpec((1, hg, dk, dv), lambda b, h, c: (b, h, 0, 0)),
    ]
    out_specs = [
        pl.BlockSpec((chunk, hg * dv), lambda b, h, c: (row(b, h, c), h)),
        pl.BlockSpec((1, hg, dk, dv), lambda b, h, c: (b, h, 0, 0)),
    ]
    return pl.pallas_call(
        functools.partial(_retention_kernel, n_chunks=n_chunks, dk=dk, dv=dv, hg=hg, precise=precise),
        grid=(batch, ng, n_chunks),
        in_specs=in_specs,
        out_specs=out_specs,
        out_shape=[jax.ShapeDtypeStruct((rows, heads * dv), F32 if precise else BF16),
                   jax.ShapeDtypeStruct((batch, heads, dk, dv), F32)],
        scratch_shapes=[pltpu.VMEM((hg, dk, dv), F32)],
        compiler_params=_cparams("arbitrary", "arbitrary", "arbitrary"),
        name="retention",
    )(qk, qk, vg, vg, cos, sin, dec, qd, kd, cd, state0)


def _top_rows(gate_t, n_valid, n_sel):
    n, r = gate_t.shape
    row = lax.broadcasted_iota(jnp.int32, (n, r), 0)
    valid = row < n_valid
    g = jnp.where(valid, gate_t, -jnp.inf)
    rank = jnp.zeros((n, r), F32)
    for j in range(n):
        gj = g[j:j + 1, :]
        tie = jnp.where(gj == g, jnp.where(row > j, 1.0, 0.0), 0.0)
        rank = rank + jnp.where(gj > g, 1.0, tie)
    return jnp.where(valid, jnp.where(rank < n_sel, 1.0, 0.0), 0.0)


def _moba_prompt_kernel(q_ref, k_ref, v_ref, o_ref, kb_ref, vt_ref, means_ref, sel_ref, m_ref, l_ref,
                        acc_ref, *, n_blocks, hd, hg):
    t = pl.program_id(2)
    blk = MOBA_BLOCK
    scale = hd ** -0.5

    @pl.when(t == 0)
    def _():
        kb_ref[...] = k_ref[...].astype(BF16)
        for n in range(n_blocks):
            rows = slice(n * blk, (n + 1) * blk)
            means_ref[n:n + 1, :] = jnp.mean(k_ref[rows, :], axis=0, keepdims=True)
            for i in range(hg):
                vt_ref[i * hd:(i + 1) * hd, rows] = jnp.transpose(v_ref[rows, i * hd:(i + 1) * hd]).astype(BF16)

    r0 = pl.multiple_of(t * blk, blk)
    ki = lax.broadcasted_iota(jnp.int32, (blk, blk), 0)
    qi = lax.broadcasted_iota(jnp.int32, (blk, blk), 1)

    heads = [slice(i * hd, (i + 1) * hd) for i in range(hg)]
    gates = [_dot_nt(means_ref[:, cs], q_ref[:, cs], precision=lax.Precision.HIGHEST) for cs in heads]
    own = [_dot_nt(kb_ref[pl.ds(r0, blk), cs], q_ref[:, cs].astype(BF16)) for cs in heads]
    for i, cs in enumerate(heads):
        sel_ref[i] = _top_rows(gates[i], t, min(MOBA_TOPK, n_blocks))
        s = jnp.where(ki <= qi, own[i] * scale, NEG_BIG)
        m0 = jnp.max(s, axis=0, keepdims=True)
        p = jnp.exp(s - m0)
        m_ref[i] = m0
        l_ref[i] = jnp.sum(p, axis=0, keepdims=True)
        acc_ref[i] = _dot(vt_ref[cs, pl.ds(r0, blk)], p.astype(BF16))

    def body(n, c):
        rn = pl.multiple_of(n * blk, blk)
        scores = [_dot_nt(kb_ref[pl.ds(rn, blk), cs], q_ref[:, cs].astype(BF16)) for cs in heads]
        upd = []
        for i, cs in enumerate(heads):
            s = jnp.where(sel_ref[i, pl.ds(n, 1), :] > 0.5, scores[i] * scale, NEG_BIG)
            m_new = jnp.maximum(m_ref[i], jnp.max(s, axis=0, keepdims=True))
            alpha = jnp.exp(m_ref[i] - m_new)
            p = jnp.exp(s - m_new)
            pv = _dot(vt_ref[cs, pl.ds(rn, blk)], p.astype(BF16))
            upd.append((m_new, alpha, jnp.sum(p, axis=0, keepdims=True), pv))
        for i, (m_new, alpha, psum, pv) in enumerate(upd):
            m_ref[i] = m_new
            l_ref[i] = alpha * l_ref[i] + psum
            acc_ref[i] = alpha * acc_ref[i] + pv
        return c

    lax.fori_loop(0, t, body, 0)
    for i in range(hg):
        o_ref[:, i * hd:(i + 1) * hd] = jnp.transpose(acc_ref[i] / l_ref[i]).astype(o_ref.dtype)


def moba_prompt(q, k, v, *, batch, seq, heads, hd):
    n_blocks = seq // MOBA_BLOCK
    assert seq % MOBA_BLOCK == 0
    hg = 4 if heads % 4 == 0 else 1
    qspec = pl.BlockSpec((MOBA_BLOCK, hg * hd), lambda b, h, t: (b * n_blocks + t, h))
    kvspec = pl.BlockSpec((seq, hg * hd), lambda b, h, t: (b, h))
    return pl.pallas_call(
        functools.partial(_moba_prompt_kernel, n_blocks=n_blocks, hd=hd, hg=hg),
        grid=(batch, heads // hg, n_blocks),
        in_specs=[qspec, kvspec, kvspec],
        out_specs=qspec,
        out_shape=jax.ShapeDtypeStruct((batch * seq, heads * hd), BF16),
        scratch_shapes=[pltpu.VMEM((seq, hg * hd), BF16), pltpu.VMEM((hg * hd, seq), BF16),
                        pltpu.VMEM((n_blocks, hg * hd), F32),
                        pltpu.VMEM((hg, n_blocks, MOBA_BLOCK), F32),
                        pltpu.VMEM((hg, 1, MOBA_BLOCK), F32), pltpu.VMEM((hg, 1, MOBA_BLOCK), F32),
                        pltpu.VMEM((hg, hd, MOBA_BLOCK), F32)],
        compiler_params=_cparams("arbitrary", "arbitrary", "arbitrary"),
        name="moba_prompt",
    )(q, k, v)


SAMPLE_ROWS = 8
MEANS_PAGES = 4


def _page_means_kernel(pt_ref, *refs, pages_per_block):
    kp = refs[:MEANS_PAGES]
    o_ref = refs[MEANS_PAGES]
    g = pl.program_id(1)
    for i in range(MEANS_PAGES // pages_per_block):
        tot = jnp.sum(kp[i * pages_per_block][...], axis=0)
        for j in range(1, pages_per_block):
            tot = tot + jnp.sum(kp[i * pages_per_block + j][...], axis=0)
        o_ref[0, g * (MEANS_PAGES // pages_per_block) + i] = tot * (1.0 / MOBA_BLOCK)


def page_block_means(cache_k, pt_flat, *, batch, n_pages):
    _, _, page, heads, hd = cache_k.shape
    ppb = MOBA_BLOCK // page
    nb = n_pages // ppb
    assert MEANS_PAGES % ppb == 0 and n_pages % MEANS_PAGES == 0

    def pspec(i):
        return pl.BlockSpec((None, None, page, heads, hd),
                            lambda b, g, pt: (0, pt[b * n_pages + g * MEANS_PAGES + i], 0, 0, 0))

    return pl.pallas_call(
        functools.partial(_page_means_kernel, pages_per_block=ppb),
        grid_spec=pltpu.PrefetchScalarGridSpec(
            num_scalar_prefetch=1,
            grid=(batch, n_pages // MEANS_PAGES),
            in_specs=[pspec(i) for i in range(MEANS_PAGES)],
            out_specs=pl.BlockSpec((1, nb, heads, hd), lambda b, g, pt: (b, 0, 0, 0)),
        ),
        out_shape=jax.ShapeDtypeStruct((batch, nb, heads, hd), F32),
        compiler_params=_cparams("arbitrary", "arbitrary"),
        name="page_means",
    )(pt_flat, *([cache_k] * MEANS_PAGES))


def _sample_gate_kernel(q_ref, m_ref, idx_ref, *, n_tok, n_sel):
    means = m_ref[0]
    nb = means.shape[0]
    blk_id = lax.broadcasted_iota(jnp.int32, (nb,) + means.shape[1:2] + (1,), 0)
    for t in range(n_tok):
        g = jnp.sum(means * q_ref[0, t][None], axis=-1, keepdims=True)
        for s in range(n_sel):
            m = jnp.max(g, axis=0, keepdims=True)
            idx = jnp.min(jnp.where(g == m, blk_id, nb), axis=0, keepdims=True)
            idx_ref[0, t * n_sel + s] = idx[0]
            g = jnp.where(blk_id == idx, -jnp.inf, g)


def sample_gate(q4, means, *, n_tok):
    batch, nb, heads, hd = means.shape
    assert nb >= MOBA_TOPK
    return pl.pallas_call(
        functools.partial(_sample_gate_kernel, n_tok=n_tok, n_sel=MOBA_TOPK),
        grid=(batch,),
        in_specs=[pl.BlockSpec((1, n_tok, heads, hd), lambda b: (b, 0, 0, 0)),
                  pl.BlockSpec((1, nb, heads, hd), lambda b: (b, 0, 0, 0))],
        out_specs=pl.BlockSpec((1, n_tok * MOBA_TOPK, heads, 1), lambda b: (b, 0, 0, 0)),
        out_shape=jax.ShapeDtypeStruct((batch, n_tok * MOBA_TOPK, heads, 1), jnp.int32),
        compiler_params=_cparams("arbitrary"),
        name="sample_gate",
    )(q4, means)


def _sample_attn_kernel(pp_ref, q_ref, kn_ref, vn_ref, ck_hbm, cv_hbm, o_ref, kbuf, vbuf, sem,
                        *, n_steps, heads, n_tok, n_pg, hd):
    step = pl.program_id(0)
    per_step = n_tok * n_pg

    def copies(st, slot):
        h = st % heads
        out = []
        for i in range(per_step):
            pg = pp_ref[st * per_step + i]
            out.append(pltpu.make_async_copy(ck_hbm.at[0, pg, :, h, :], kbuf.at[slot, i], sem.at[0, slot]))
            out.append(pltpu.make_async_copy(cv_hbm.at[0, pg, :, h, :], vbuf.at[slot, i], sem.at[1, slot]))
        return out

    slot = step % 2

    @pl.when(step == 0)
    def _():
        for cp in copies(step, 0):
            cp.start()

    @pl.when(step + 1 < n_steps)
    def _():
        for cp in copies(step + 1, 1 - slot):
            cp.start()

    for cp in copies(step, slot):
        cp.wait()

    r = SAMPLE_ROWS
    scale = hd ** -0.5
    kn = kn_ref[...]
    vn = vn_ref[...]
    row = lax.broadcasted_iota(jnp.int32, (r, 1), 0)
    o_ref[...] = jnp.zeros_like(o_ref)
    for t in range(n_tok):
        q_t = q_ref[t:t + 1, :]
        s_own = jnp.sum(kn * q_t, axis=-1, keepdims=True) * scale
        s_own = jnp.where(row <= t, s_own, NEG_BIG)
        s_pg = [jnp.sum(kbuf[slot, t * n_pg + j] * q_t, axis=-1, keepdims=True) * scale for j in range(n_pg)]
        m = jnp.max(s_own, axis=0, keepdims=True)
        for s in s_pg:
            m = jnp.maximum(m, jnp.max(s, axis=0, keepdims=True))
        p_own = jnp.exp(s_own - m)
        l = jnp.sum(p_own, axis=0, keepdims=True)
        acc = jnp.sum(p_own * vn, axis=0, keepdims=True)
        for j in range(n_pg):
            p = jnp.exp(s_pg[j] - m)
            l = l + jnp.sum(p, axis=0, keepdims=True)
            acc = acc + jnp.sum(p * vbuf[slot, t * n_pg + j], axis=0, keepdims=True)
        o_ref[t:t + 1, :] = acc / l


def sample_attention(q8, kn8, vn8, cache_k, cache_v, pp_flat, *, batch, heads, hd, n_tok, page):
    ppb = MOBA_BLOCK // page
    n_pg = MOBA_TOPK * ppb
    n_steps = batch * heads
    small = pl.BlockSpec((SAMPLE_ROWS, hd), lambda s, pp: (s // heads, s % heads))
    return pl.pallas_call(
        functools.partial(_sample_attn_kernel, n_steps=n_steps, heads=heads, n_tok=n_tok, n_pg=n_pg, hd=hd),
        grid_spec=pltpu.PrefetchScalarGridSpec(
            num_scalar_prefetch=1,
            grid=(n_steps,),
            in_specs=[small, small, small, pl.BlockSpec(memory_space=pl.ANY), pl.BlockSpec(memory_space=pl.ANY)],
            out_specs=small,
            scratch_shapes=[pltpu.VMEM((2, n_tok * n_pg, page, hd), F32),
                            pltpu.VMEM((2, n_tok * n_pg, page, hd), F32),
                            pltpu.SemaphoreType.DMA((2, 2))],
        ),
        out_shape=jax.ShapeDtypeStruct((batch * SAMPLE_ROWS, heads * hd), F32),
        compiler_params=_cparams("arbitrary"),
        name="sample_attn",
    )(pp_flat, q8, kn8, vn8, cache_k, cache_v)


def _norm_router_kernel(x_ref, w_ref, wr_ref, h_ref, info_ref, *, n_experts):
    x = x_ref[...]
    y = x * lax.rsqrt(jnp.mean(x * x, axis=-1, keepdims=True) + RMS_EPS) * w_ref[...]
    h_ref[...] = y
    logits = jnp.dot(y, wr_ref[...], precision=lax.Precision.HIGHEST, preferred_element_type=F32)
    r, n = logits.shape
    col = lax.broadcasted_iota(jnp.int32, (r, n), 1)
    lg = jnp.where(col < n_experts, logits, -jnp.inf)
    m1 = jnp.max(lg, axis=-1, keepdims=True)
    i1 = jnp.min(jnp.where(lg == m1, col, n), axis=-1, keepdims=True)
    lg2 = jnp.where(col == i1, -jnp.inf, lg)
    m2 = jnp.max(lg2, axis=-1, keepdims=True)
    i2 = jnp.min(jnp.where(lg2 == m2, col, n), axis=-1, keepdims=True)
    e2 = jnp.exp(m2 - m1)
    den = 1.0 + e2
    w1 = 1.0 / den
    w2 = e2 / den
    info = jnp.where(col == 0, i1.astype(F32),
                     jnp.where(col == 1, i2.astype(F32),
                               jnp.where(col == 2, w1, jnp.where(col == 3, w2, 0.0))))
    info_ref[...] = info


def norm_router(x, w, w_router):
    m, d = x.shape
    e = w_router.shape[1]
    tm = _pick(m, (256, 128, 64, 32, 16, 8))
    wr = jnp.zeros((d, 128), F32).at[:, :e].set(w_router)
    return pl.pallas_call(
        functools.partial(_norm_router_kernel, n_experts=e),
        grid=(m // tm,),
        in_specs=[pl.BlockSpec((tm, d), lambda i: (i, 0)), pl.BlockSpec((1, d), lambda i: (0, 0)),
                  pl.BlockSpec((d, 128), lambda i: (0, 0))],
        out_specs=[pl.BlockSpec((tm, d), lambda i: (i, 0)), pl.BlockSpec((tm, 128), lambda i: (i, 0))],
        out_shape=[jax.ShapeDtypeStruct((m, d), F32), jax.ShapeDtypeStruct((m, 128), F32)],
        compiler_params=_cparams("parallel"),
        name="norm_router",
    )(x, w.reshape(1, d), wr)


GATHER_TILE = 256


def _start_rows(src_hbm, dst_ref, sem, row_of):
    def start(r, c):
        pltpu.make_async_copy(src_hbm.at[pl.ds(row_of(r), 1), :], dst_ref.at[pl.ds(r, 1), :], sem).start()
        return c

    lax.fori_loop(0, dst_ref.shape[0], start, 0, unroll=8)


def _wait_rows(dst_ref, sem):
    pltpu.make_async_copy(dst_ref, dst_ref, sem).wait()


def _gather_cast_kernel(src_ref, ntile_ref, h_hbm, o_ref, buf_ref, sem):
    i = pl.program_id(0)
    n = ntile_ref[0]

    def issue(tile, slot):
        base = tile * GATHER_TILE
        _start_rows(h_hbm, buf_ref.at[slot], sem.at[slot], lambda r: src_ref[base + r])

    @pl.when(jnp.logical_and(i == 0, n > 0))
    def _():
        issue(0, 0)

    @pl.when(i + 1 < n)
    def _():
        issue(i + 1, (i + 1) % 2)

    @pl.when(i < n)
    def _():
        _wait_rows(buf_ref.at[i % 2], sem.at[i % 2])
        o_ref[...] = buf_ref[i % 2].astype(o_ref.dtype)

    @pl.when(i >= n)
    def _():
        o_ref[...] = jnp.zeros_like(o_ref)


def gather_cast(h, src_rows, n_tiles_used):
    s_rows = src_rows.shape[0]
    d = h.shape[1]
    return pl.pallas_call(
        _gather_cast_kernel,
        grid_spec=pltpu.PrefetchScalarGridSpec(
            num_scalar_prefetch=2,
            grid=(s_rows // GATHER_TILE,),
            in_specs=[pl.BlockSpec(memory_space=pl.ANY)],
            out_specs=pl.BlockSpec((GATHER_TILE, d), lambda i, s, n: (i, 0)),
            scratch_shapes=[pltpu.VMEM((2, GATHER_TILE, d), F32), pltpu.SemaphoreType.DMA((2,))],
        ),
        out_shape=jax.ShapeDtypeStruct((s_rows, d), BF16),
        compiler_params=_cparams("arbitrary"),
        name="gather_cast",
    )(src_rows, n_tiles_used, h)


EXPERT_SUB = 128
EXPERT_MAIN = 8
SEGMENT_ROWS = 2304


def _experts_kernel(seg_e_ref, seg_rows_ref, nseg_ref, x_ref, wg_ref, wu_ref, wd_ref, o_ref,
                    wgb_ref, wub_ref, wdb_ref):
    s = pl.program_id(0)
    f = pl.program_id(1)

    @pl.when(f == 0)
    def _():
        o_ref[...] = jnp.zeros_like(o_ref)

    def tiles(starts, size):
        gu = []
        for r0 in starts:
            xs = x_ref[pl.ds(r0, size), :]
            gu.append((_dot(xs, wgb_ref[...]), _dot(xs, wub_ref[...])))
        for r0, (g, u) in zip(starts, gu):
            a = (_silu(g) * u).astype(BF16)
            o_ref[pl.ds(r0, size), :] += _dot(a, wdb_ref[...])

    @pl.when(s < nseg_ref[0])
    def _():
        wgb_ref[...] = wg_ref[...].astype(BF16)
        wub_ref[...] = wu_ref[...].astype(BF16)
        wdb_ref[...] = wd_ref[...].astype(BF16)
        sub = EXPERT_SUB
        n_sub = (seg_rows_ref[s] + sub - 1) // sub
        n_main = n_sub // EXPERT_MAIN

        def body(i, c):
            r0 = pl.multiple_of(i * (EXPERT_MAIN * sub), EXPERT_MAIN * sub)
            half = EXPERT_MAIN // 2 * sub
            tiles([r0, r0 + half], half)
            return c

        lax.fori_loop(0, n_main, body, 0)
        done = n_main * EXPERT_MAIN
        size = EXPERT_MAIN // 2
        while size >= 1:
            take = ((n_sub - done) // size) > 0

            @pl.when(take)
            def _(done=done, size=size):
                tiles([pl.multiple_of(done * sub, sub)], size * sub)

            done = done + jnp.where(take, size, 0)
            size //= 2


def experts(x_sorted, wg, wu, wd, layer, seg_e, seg_rows, nseg, *, seg_cap):
    s_rows, d = x_sorted.shape
    s_max = s_rows // seg_cap
    f_dim = wg.shape[3]
    tf = _pick(f_dim, (256, 128))
    nf = f_dim // tf

    def seg_blk(s, n):
        return jnp.minimum(s, n[0] - 1)

    def fblk(s, f, n):
        return jnp.where(s < n[0], f, nf - 1)

    one = pl.Buffered(1)
    in_specs = [
        pl.BlockSpec((seg_cap, d), lambda s, f, e, r, n: (seg_blk(s, n), 0), pipeline_mode=one),
        pl.BlockSpec((None, None, d, tf), lambda s, f, e, r, n: (layer, e[s], 0, fblk(s, f, n))),
        pl.BlockSpec((None, None, d, tf), lambda s, f, e, r, n: (layer, e[s], 0, fblk(s, f, n))),
        pl.BlockSpec((None, None, tf, d), lambda s, f, e, r, n: (layer, e[s], fblk(s, f, n), 0)),
    ]
    return pl.pallas_call(
        _experts_kernel,
        grid_spec=pltpu.PrefetchScalarGridSpec(
            num_scalar_prefetch=3,
            grid=(s_max, nf),
            in_specs=in_specs,
            out_specs=pl.BlockSpec((seg_cap, d), lambda s, f, e, r, n: (s, 0), pipeline_mode=one),
            scratch_shapes=[pltpu.VMEM((d, tf), BF16), pltpu.VMEM((d, tf), BF16),
                            pltpu.VMEM((tf, d), BF16)],
        ),
        out_shape=jax.ShapeDtypeStruct((s_rows, d), F32),
        compiler_params=_cparams("arbitrary", "arbitrary"),
        name="experts",
    )(seg_e, seg_rows, nseg, x_sorted, wg, wu, wd)


def _combine_norm_kernel(s1_ref, s2_ref, ys_hbm, x_ref, info_ref, w_ref, o_ref, a_ref, b_ref, sem,
                         *, n_steps):
    i = pl.program_id(0)
    tm = x_ref.shape[0]

    def issue(tile, slot):
        base = tile * tm
        _start_rows(ys_hbm, a_ref.at[slot], sem.at[0, slot], lambda r: s1_ref[base + r])
        _start_rows(ys_hbm, b_ref.at[slot], sem.at[1, slot], lambda r: s2_ref[base + r])

    @pl.when(i == 0)
    def _():
        issue(0, 0)

    @pl.when(i + 1 < n_steps)
    def _():
        issue(i + 1, (i + 1) % 2)

    slot = i % 2
    _wait_rows(a_ref.at[slot], sem.at[0, slot])
    _wait_rows(b_ref.at[slot], sem.at[1, slot])
    info = info_ref[...]
    y = info[:, 2:3] * a_ref[slot] + info[:, 3:4] * b_ref[slot]
    x = x_ref[...] + y
    xn = x * lax.rsqrt(jnp.mean(x * x, axis=-1, keepdims=True) + RMS_EPS)
    o_ref[...] = xn * w_ref[...]


def combine_norm(x, y_sorted, slot1, slot2, info, w):
    m, d = x.shape
    tm = _pick(m, (256, 128, 64, 32, 16, 8))
    return pl.pallas_call(
        functools.partial(_combine_norm_kernel, n_steps=m // tm),
        grid_spec=pltpu.PrefetchScalarGridSpec(
            num_scalar_prefetch=2,
            grid=(m // tm,),
            in_specs=[pl.BlockSpec(memory_space=pl.ANY),
                      pl.BlockSpec((tm, d), lambda i, a, b: (i, 0)),
                      pl.BlockSpec((tm, 128), lambda i, a, b: (i, 0)),
                      pl.BlockSpec((1, d), lambda i, a, b: (0, 0))],
            out_specs=pl.BlockSpec((tm, d), lambda i, a, b: (i, 0)),
            scratch_shapes=[pltpu.VMEM((2, tm, d), F32), pltpu.VMEM((2, tm, d), F32),
                            pltpu.SemaphoreType.DMA((2, 2))],
        ),
        out_shape=jax.ShapeDtypeStruct((m, d), F32),
        compiler_params=_cparams("arbitrary"),
        name="combine_norm",
    )(slot1, slot2, y_sorted, x, info, w.reshape(1, d))


def _route_tables(info, n_real, n_experts, seg_cap):
    m = info.shape[0]
    e = info[:, :MOE_TOPK].astype(jnp.int32)
    valid = (jnp.arange(m) < n_real)[:, None]
    onehot = ((e[:, :, None] == jnp.arange(n_experts)[None, None, :]) & valid[:, :, None])
    onehot = onehot.reshape(m * MOE_TOPK, n_experts).astype(jnp.int32)
    rank = jnp.sum((jnp.cumsum(onehot, axis=0) - onehot) * onehot, axis=1)
    counts = jnp.sum(onehot, axis=0)
    nseg_e = (counts + seg_cap - 1) // seg_cap
    seg_start = jnp.cumsum(nseg_e) - nseg_e
    nseg = jnp.sum(nseg_e)
    s_max = (m * MOE_TOPK) // seg_cap + n_experts
    flat_e = e.reshape(-1)
    flat_valid = jnp.broadcast_to(valid, (m, MOE_TOPK)).reshape(-1)
    slot = (seg_start[flat_e] + rank // seg_cap) * seg_cap + rank % seg_cap
    slot = jnp.where(flat_valid, slot, 0)
    sid = jnp.arange(s_max)
    owner = jnp.argmax((sid[:, None] >= seg_start[None, :]) & (sid[:, None] < (seg_start + nseg_e)[None, :]), axis=1)
    seg_e = jnp.where(sid < nseg, owner, owner[jnp.maximum(nseg - 1, 0)]).astype(jnp.int32)
    within = sid - seg_start[seg_e]
    seg_rows = jnp.clip(counts[seg_e] - within * seg_cap, 0, seg_cap)
    seg_rows = jnp.where(sid < nseg, seg_rows, 0).astype(jnp.int32)
    tok = jnp.repeat(jnp.arange(m, dtype=jnp.int32), MOE_TOPK)
    src = jnp.full((s_max * seg_cap,), m - 1, jnp.int32)
    src = src.at[jnp.where(flat_valid, slot, s_max * seg_cap)].set(tok, mode="drop")
    n_tiles_used = (nseg * (seg_cap // GATHER_TILE)).astype(jnp.int32).reshape(1)
    slots = slot.reshape(m, MOE_TOPK).astype(jnp.int32)
    return src, n_tiles_used, seg_e, seg_rows, nseg.astype(jnp.int32).reshape(1), slots[:, 0], slots[:, 1]


def kernel(x_prompt, x_sample, state_ret, cache_k, cache_v, page_table, norm_w, final_norm_w,
           ret_w_in, ret_w_out, moba_w_qkv, moba_w_out, ffn_w_gate, ffn_w_up, ffn_w_down,
           moe_w_router, moe_w_gate, moe_w_up, moe_w_down):
    bp, sp, d = x_prompt.shape
    bs, ts, _ = x_sample.shape
    _, _, r_heads, dk, dv = state_ret.shape
    _, n_pool, page, m_heads, hd = cache_k.shape
    n_pages = page_table.shape[1]
    past_len = n_pages * page
    n_experts = moe_w_router.shape[2]
    assert sp % RET_CHUNK == 0 and ts < RET_CHUNK and ts <= SAMPLE_ROWS
    assert MOBA_BLOCK % page == 0 and past_len % MOBA_BLOCK == 0 and sp % MOBA_BLOCK == 0

    n_p = bp * sp
    n_s = bs * ts
    n_real = n_p + n_s
    xp = x_prompt.reshape(n_p, d)
    xs = x_sample.reshape(n_s, d)

    hp = rmsnorm(xp, norm_w[0, 0], BF16)
    hs = rmsnorm(xs, norm_w[0, 0], F32)
    n_qk = 2 * r_heads * dk
    qk = matmul(hp, ret_w_in, 0, n_off=0, n=n_qk, name="ret_qk_proj")
    vg = matmul(hp, ret_w_in, 0, n_off=n_qk, n=2 * r_heads * dv, out_dtype=BF16, name="ret_vg_proj")
    pos_p = jnp.arange(sp, dtype=jnp.int32)
    zero_state = jnp.zeros((bp, r_heads, dk, dv), F32)
    y_p, st_p = retention(qk, vg, zero_state, pos_p, batch=bp, n_chunks=sp // RET_CHUNK,
                          chunk=RET_CHUNK, valid=RET_CHUNK, heads=r_heads, dk=dk, dv=dv)
    qkvg_s = matmul_precise(hs, ret_w_in, 0, name="ret_in_proj_s")

    def pad_chunk(a2):
        a3 = jnp.pad(a2.reshape(bs, ts, -1), ((0, 0), (0, RET_CHUNK - ts), (0, 0)))
        return a3.reshape(bs * RET_CHUNK, -1)

    pos_s = past_len + jnp.arange(RET_CHUNK, dtype=jnp.int32)
    y_s, st_s = retention(pad_chunk(qkvg_s[:, :n_qk]), pad_chunk(qkvg_s[:, n_qk:]), state_ret[0], pos_s,
                          batch=bs, n_chunks=1, chunk=RET_CHUNK, valid=ts, heads=r_heads, dk=dk, dv=dv,
                          precise=True)
    y_s = y_s.reshape(bs, RET_CHUNK, -1)[:, :ts].reshape(n_s, -1)
    xp = matmul(y_p, ret_w_out, 0, res=xp, name="ret_out_proj")
    xs = matmul_precise(y_s, ret_w_out, 0, res=xs, name="ret_out_proj_s")

    hp = rmsnorm(xp, norm_w[0, 1], BF16)
    hs = rmsnorm(xs, norm_w[0, 1], F32)
    xp = matmul(gate_up(hp, ffn_w_gate, ffn_w_up, 0), ffn_w_down, 0, res=xp, name="ffn_down")
    xs = matmul_precise(gate_up_precise(hs, ffn_w_gate, ffn_w_up, 0), ffn_w_down, 0, res=xs, name="ffn_down_s")

    hp = rmsnorm(xp, norm_w[1, 0], BF16)
    hs = rmsnorm(xs, norm_w[1, 0], F32)
    q = matmul(hp, moba_w_qkv, 0, n_off=0, n=d, name="moba_q_proj")
    k = matmul(hp, moba_w_qkv, 0, n_off=d, n=d, name="moba_k_proj")
    v = matmul(hp, moba_w_qkv, 0, n_off=2 * d, n=d, name="moba_v_proj")
    o_p = moba_prompt(q, k, v, batch=bp, seq=sp, heads=m_heads, hd=hd)
    qkv_s = matmul_precise(hs, moba_w_qkv, 0, name="moba_qkv_proj_s")
    q_s, k_s, v_s = qkv_s[:, :d], qkv_s[:, d:2 * d], qkv_s[:, 2 * d:]

    def pad8(a2):
        return jnp.pad(a2.reshape(bs, ts, d), ((0, 0), (0, SAMPLE_ROWS - ts), (0, 0))).reshape(bs * SAMPLE_ROWS, d)

    means = page_block_means(cache_k, page_table.reshape(-1), batch=bs, n_pages=n_pages)
    sel = sample_gate(q_s.reshape(bs, ts, m_heads, hd), means, n_tok=ts)
    sel = sel.reshape(bs, ts, MOBA_TOPK, m_heads).transpose(0, 3, 1, 2)
    ppb = MOBA_BLOCK // page
    lpage = sel[..., None] * ppb + jnp.arange(ppb, dtype=jnp.int32)
    pp = page_table[jnp.arange(bs)[:, None, None, None], lpage.reshape(bs, m_heads, ts, MOBA_TOPK * ppb)]
    o_s8 = sample_attention(pad8(q_s), pad8(k_s), pad8(v_s), cache_k, cache_v, pp.reshape(-1).astype(jnp.int32),
                            batch=bs, heads=m_heads, hd=hd, n_tok=ts, page=page)
    o_s = o_s8.reshape(bs, SAMPLE_ROWS, d)[:, :ts].reshape(n_s, d)
    xp = matmul(o_p, moba_w_out, 0, res=xp, name="moba_out_proj")
    xs = matmul_precise(o_s, moba_w_out, 0, res=xs, name="moba_out_proj_s")

    m = -(-(n_real + 1) // ROW_ALIGN) * ROW_ALIGN
    x = jnp.concatenate([xp, xs, jnp.zeros((m - n_real, d), F32)], axis=0)
    hf, info = norm_router(x, norm_w[1, 1], moe_w_router[0])
    seg_cap = SEGMENT_ROWS if m * MOE_TOPK >= 4 * SEGMENT_ROWS else 2 * GATHER_TILE
    src, n_tiles_used, seg_e, seg_rows, nseg, slot1, slot2 = _route_tables(info, n_real, n_experts, seg_cap)
    x_sorted = gather_cast(hf, src, n_tiles_used)
    y_sorted = experts(x_sorted, moe_w_gate, moe_w_up, moe_w_down, 0, seg_e, seg_rows, nseg, seg_cap=seg_cap)
    y = combine_norm(x, y_sorted, slot1, slot2, info, final_norm_w)

    y_prompt = y[:n_p].reshape(bp, sp, d)
    y_sample = y[n_p:n_real].reshape(bs, ts, d)
    kv_shape_p = (1, bp, sp, m_heads, hd)
    kv_shape_s = (1, bs, ts, m_heads, hd)
    return (y_prompt, y_sample, st_p[None], st_s[None],
            k.reshape(kv_shape_p), v.reshape(kv_shape_p), k_s.reshape(kv_shape_s), v_s.reshape(kv_shape_s))
```

```python
import functools

import jax
import jax.numpy as jnp
from jax import lax
from jax.experimental import pallas as pl
from jax.experimental.pallas import tpu as pltpu

RET_CHUNK = 128
ROPE_BASE = 10000.0
MOBA_BLOCK = 256
MOBA_TOPK = 3
MOE_TOPK = 2
RMS_EPS = 1e-6
GN_EPS = 1e-5

V7X_VMEM_LIMIT_BYTES = 56 * 1024 * 1024
ROW_ALIGN = 256
NEG_BIG = -1e30

F32 = jnp.float32
BF16 = jnp.bfloat16


def _cparams(*sem):
    return pltpu.CompilerParams(dimension_semantics=sem, vmem_limit_bytes=V7X_VMEM_LIMIT_BYTES)


def _pick(n, prefs):
    for p in prefs:
        if p <= n and n % p == 0:
            return p
    return n


def _silu(g):
    return g * (1.0 / (1.0 + jnp.exp(-g)))


def _dot(a, b):
    return jnp.dot(a, b, preferred_element_type=F32)


def _dot_nt(a, b, precision=None):
    return lax.dot_general(a, b, (((1,), (1,)), ((), ())), precision=precision,
                           preferred_element_type=F32)


def _split(x):
    hi = x.astype(BF16)
    return hi, (x - hi.astype(F32)).astype(BF16)


def _dot3(a, b):
    ah, al = _split(a)
    bh, bl = _split(b)
    return _dot(ah, bh) + (_dot(al, bh) + _dot(ah, bl))


def _dot3_nt(a, b):
    ah, al = _split(a)
    bh, bl = _split(b)
    return _dot_nt(ah, bh) + (_dot_nt(al, bh) + _dot_nt(ah, bl))


def _dot1(a, b):
    return _dot(a.astype(BF16), b.astype(BF16))


def _dot1_nt(a, b):
    return _dot_nt(a.astype(BF16), b.astype(BF16))


def _rmsnorm_kernel(x_ref, w_ref, o_ref):
    x = x_ref[...]
    y = x * lax.rsqrt(jnp.mean(x * x, axis=-1, keepdims=True) + RMS_EPS)
    o_ref[...] = (y * w_ref[...]).astype(o_ref.dtype)


def rmsnorm(x, w, out_dtype):
    m, d = x.shape
    tm = _pick(m, (256, 128, 64, 32, 16, 8))
    return pl.pallas_call(
        _rmsnorm_kernel,
        grid=(m // tm,),
        in_specs=[pl.BlockSpec((tm, d), lambda i: (i, 0)), pl.BlockSpec((1, d), lambda i: (0, 0))],
        out_specs=pl.BlockSpec((tm, d), lambda i: (i, 0)),
        out_shape=jax.ShapeDtypeStruct((m, d), out_dtype),
        compiler_params=_cparams("parallel"),
        name="rmsnorm",
    )(x, w.reshape(1, d))


def _mm_kernel(*refs, has_res):
    if has_res:
        x_ref, w_ref, r_ref, o_ref, wb_ref = refs
    else:
        x_ref, w_ref, o_ref, wb_ref = refs

    @pl.when(pl.program_id(1) == 0)
    def _():
        wb_ref[...] = w_ref[...].astype(BF16)

    acc = _dot(x_ref[...], wb_ref[...])
    if has_res:
        acc = acc + r_ref[...]
    o_ref[...] = acc.astype(o_ref.dtype)


TILE_VMEM_BUDGET = 44 * 1024 * 1024


def _matmul_tiles(m, k, n, n_w, out_bytes, has_res):
    best = None
    for tm in (2048, 1056, 1024, 528, 512, 256, 128, 64, 32, 16):
        if m % tm:
            continue
        for tn in (1024, 512, 256, 128):
            if n % tn:
                continue
            need = (2 * tm * k * 2 + n_w * (2 * k * tn * 4 + k * tn * 2)
                    + 2 * tm * tn * out_bytes + (2 * tm * tn * 4 if has_res else 0))
            score = (-(m // tm) * (n // tn), tn)
            if need <= TILE_VMEM_BUDGET and (best is None or score > best[0]):
                best = (score, (tm, tn))
    assert best is not None
    return best[1]


def matmul(x, w3, layer, *, name, n_off=0, n=None, res=None, out_dtype=F32):
    m, k = x.shape
    n = w3.shape[2] if n is None else n
    tm, tn = _matmul_tiles(m, k, n, 1, jnp.dtype(out_dtype).itemsize, res is not None)
    assert n_off % tn == 0
    off = n_off // tn
    in_specs = [pl.BlockSpec((tm, k), lambda j, i: (i, 0)),
                pl.BlockSpec((None, k, tn), lambda j, i: (layer, 0, j + off))]
    args = [x, w3]
    if res is not None:
        in_specs.append(pl.BlockSpec((tm, tn), lambda j, i: (i, j)))
        args.append(res)
    return pl.pallas_call(
        functools.partial(_mm_kernel, has_res=res is not None),
        grid=(n // tn, m // tm),
        in_specs=in_specs,
        out_specs=pl.BlockSpec((tm, tn), lambda j, i: (i, j)),
        out_shape=jax.ShapeDtypeStruct((m, n), out_dtype),
        scratch_shapes=[pltpu.VMEM((k, tn), BF16)],
        compiler_params=_cparams("arbitrary", "arbitrary"),
        name=name,
    )(*args)


def _gateup_kernel(x_ref, wg_ref, wu_ref, o_ref, wgb_ref, wub_ref):
    @pl.when(pl.program_id(1) == 0)
    def _():
        wgb_ref[...] = wg_ref[...].astype(BF16)
        wub_ref[...] = wu_ref[...].astype(BF16)

    x = x_ref[...]
    g = _dot(x, wgb_ref[...])
    u = _dot(x, wub_ref[...])
    o_ref[...] = (_silu(g) * u).astype(o_ref.dtype)


def gate_up(x, wg3, wu3, layer):
    m, k = x.shape
    f = wg3.shape[2]
    tm, tn = _matmul_tiles(m, k, f, 2, 2, False)
    wspec = pl.BlockSpec((None, k, tn), lambda j, i: (layer, 0, j))
    return pl.pallas_call(
        _gateup_kernel,
        grid=(f // tn, m // tm),
        in_specs=[pl.BlockSpec((tm, k), lambda j, i: (i, 0)), wspec, wspec],
        out_specs=pl.BlockSpec((tm, tn), lambda j, i: (i, j)),
        out_shape=jax.ShapeDtypeStruct((m, f), BF16),
        scratch_shapes=[pltpu.VMEM((k, tn), BF16), pltpu.VMEM((k, tn), BF16)],
        compiler_params=_cparams("arbitrary", "arbitrary"),
        name="gate_up",
    )(x, wg3, wu3)


PRECISE_W_TILE_BYTES = 8 * 1024 * 1024


def _precise_tn(k, n, n_w=1):
    fits = [tn for tn in range(128, n + 1, 128) if n % tn == 0 and n_w * k * tn * 4 <= PRECISE_W_TILE_BYTES]
    assert fits
    return fits[-1]


def _mm3_kernel(*refs, has_res):
    if has_res:
        x_ref, w_ref, r_ref, o_ref = refs
    else:
        x_ref, w_ref, o_ref = refs
    acc = _dot3(x_ref[...], w_ref[...])
    if has_res:
        acc = acc + r_ref[...]
    o_ref[...] = acc


def matmul_precise(x, w3, layer, *, name, n_off=0, n=None, res=None):
    m, k = x.shape
    n = w3.shape[2] if n is None else n
    tn = _precise_tn(k, n)
    assert n_off % tn == 0
    off = n_off // tn
    in_specs = [pl.BlockSpec((m, k), lambda j: (0, 0)),
                pl.BlockSpec((None, k, tn), lambda j: (layer, 0, j + off))]
    args = [x, w3]
    if res is not None:
        in_specs.append(pl.BlockSpec((m, tn), lambda j: (0, j)))
        args.append(res)
    return pl.pallas_call(
        functools.partial(_mm3_kernel, has_res=res is not None),
        grid=(n // tn,),
        in_specs=in_specs,
        out_specs=pl.BlockSpec((m, tn), lambda j: (0, j)),
        out_shape=jax.ShapeDtypeStruct((m, n), F32),
        compiler_params=_cparams("arbitrary"),
        name=name,
    )(*args)


def _gateup3_kernel(x_ref, wg_ref, wu_ref, o_ref):
    x = x_ref[...]
    g = _dot3(x, wg_ref[...])
    u = _dot3(x, wu_ref[...])
    o_ref[...] = _silu(g) * u


def gate_up_precise(x, wg3, wu3, layer):
    m, k = x.shape
    f = wg3.shape[2]
    tn = _precise_tn(k, f, 2)
    wspec = pl.BlockSpec((None, k, tn), lambda j: (layer, 0, j))
    return pl.pallas_call(
        _gateup3_kernel,
        grid=(f // tn,),
        in_specs=[pl.BlockSpec((m, k), lambda j: (0, 0)), wspec, wspec],
        out_specs=pl.BlockSpec((m, tn), lambda j: (0, j)),
        out_shape=jax.ShapeDtypeStruct((m, f), F32),
        compiler_params=_cparams("arbitrary"),
        name="gate_up_precise",
    )(x, wg3, wu3)


def _retention_kernel(q_ref, k_ref, v_ref, g_ref, cos_ref, sin_ref, dec_ref, qd_ref, kd_ref,
                      cd_ref, s0_ref, y_ref, sout_ref, state_ref, *, n_chunks, dk, dv, hg, precise):
    c = pl.program_id(2)

    @pl.when(c == 0)
    def _():
        state_ref[...] = s0_ref[0]

    half = dk // 2
    cos = cos_ref[...]
    sin = sin_ref[...]

    def rot(x):
        x1, x2 = x[:, :half], x[:, half:]
        return jnp.concatenate([x1 * cos - x2 * sin, x1 * sin + x2 * cos], axis=-1)

    qs, ks, vs, sts = [], [], [], []
    for i in range(hg):
        qs.append(rot(q_ref[:, i * dk:(i + 1) * dk]))
        ks.append(rot(k_ref[:, i * dk:(i + 1) * dk]) * (dk ** -0.5))
        vs.append(v_ref[:, i * dv:(i + 1) * dv])
        sts.append(state_ref[i])
    mm, mm_nt = (_dot3, _dot3_nt) if precise else (_dot1, _dot1_nt)
    raw = [mm_nt(qs[i], ks[i]) for i in range(hg)]
    crosses = [mm(qs[i], sts[i]) for i in range(hg)]
    for i in range(hg):
        kdt = jnp.transpose(ks[i] * kd_ref[i])
        state_ref[i] = sts[i] * cd_ref[i] + mm(kdt, vs[i])
    inners = [mm(raw[i] * dec_ref[i], vs[i]) for i in range(hg)]
    for i in range(hg):
        y = inners[i] + crosses[i] * qd_ref[i]
        mu = jnp.mean(y, axis=-1, keepdims=True)
        d = y - mu
        var = jnp.mean(d * d, axis=-1, keepdims=True)
        yn = d * lax.rsqrt(var + GN_EPS)
        g = g_ref[:, i * dv:(i + 1) * dv].astype(F32)
        y_ref[:, i * dv:(i + 1) * dv] = (_silu(g) * yn).astype(y_ref.dtype)

    @pl.when(c == n_chunks - 1)
    def _():
        sout_ref[0] = state_ref[...]


def _retention_tables(heads, chunk, valid):
    lg = jnp.log1p(-jnp.exp2(-5.0 - jnp.arange(heads, dtype=F32)))
    i = jnp.arange(chunk, dtype=F32)
    ok = i < valid
    diff = i[:, None] - i[None, :]
    dec = jnp.where((diff[None] >= 0) & ok[None, :, None] & ok[None, None, :],
                    jnp.exp(jnp.maximum(diff, 0.0)[None] * lg[:, None, None]), 0.0)
    qd = jnp.where(ok[None, :], jnp.exp((i + 1.0)[None, :] * lg[:, None]), 0.0)
    kd = jnp.where(ok[None, :], jnp.exp((valid - 1.0 - i)[None, :] * lg[:, None]), 0.0)
    cd = jnp.exp(valid * lg)
    return dec, qd[:, :, None], kd[:, :, None], cd[:, None, None]


def _rope_tables(pos, half):
    inv = ROPE_BASE ** (-jnp.arange(half, dtype=F32) / half)
    ang = pos.astype(F32)[:, None] * inv[None, :]
    return jnp.cos(ang), jnp.sin(ang)


def retention(qk, vg, state0, pos, *, batch, n_chunks, chunk, valid, heads, dk, dv, precise=False):
    rows = batch * n_chunks * chunk
    assert qk.shape[0] >= rows and vg.shape[0] >= rows
    hg = 2 if heads % 2 == 0 else 1
    ng = heads // hg
    dec, qd, kd, cd = _retention_tables(heads, chunk, valid)
    cos, sin = _rope_tables(pos, dk // 2)
    row = lambda b, h, c: b * n_chunks + c
    in_specs = [
        pl.BlockSpec((chunk, hg * dk), lambda b, h, c: (row(b, h, c), h)),
        pl.BlockSpec((chunk, hg * dk), lambda b, h, c: (row(b, h, c), ng + h)),
        pl.BlockSpec((chunk, hg * dv), lambda b, h, c: (row(b, h, c), h)),
        pl.BlockSpec((chunk, hg * dv), lambda b, h, c: (row(b, h, c), ng + h)),
        pl.BlockSpec((chunk, dk // 2), lambda b, h, c: (c, 0)),
        pl.BlockSpec((chunk, dk // 2), lambda b, h, c: (c, 0)),
        pl.BlockSpec((hg, chunk, chunk), lambda b, h, c: (h, 0, 0)),
        pl.BlockSpec((hg, chunk, 1), lambda b, h, c: (h, 0, 0)),
        pl.BlockSpec((hg, chunk, 1), lambda b, h, c: (h, 0, 0)),
        pl.BlockSpec((hg, 1, 1), lambda b, h, c: (h, 0, 0)),
        pl.BlockSpec((1, hg, dk, dv), lambda b, h, c: (b, h, 0, 0)),
    ]
    out_specs = [
        pl.BlockSpec((chunk, hg * dv), lambda b, h, c: (row(b, h, c), h)),
        pl.BlockSpec((1, hg, dk, dv), lambda b, h, c: (b, h, 0, 0)),
    ]
    return pl.pallas_call(
        functools.partial(_retention_kernel, n_chunks=n_chunks, dk=dk, dv=dv, hg=hg, precise=precise),
        grid=(batch, ng, n_chunks),
        in_specs=in_specs,
        out_specs=out_specs,
        out_shape=[jax.ShapeDtypeStruct((rows, heads * dv), F32 if precise else BF16),
                   jax.ShapeDtypeStruct((batch, heads, dk, dv), F32)],
        scratch_shapes=[pltpu.VMEM((hg, dk, dv), F32)],
        compiler_params=_cparams("arbitrary", "arbitrary", "arbitrary"),
        name="retention",
    )(qk, qk, vg, vg, cos, sin, dec, qd, kd, cd, state0)


def _top_rows(gate_t, n_valid, n_sel):
    n, r = gate_t.shape
    row = lax.broadcasted_iota(jnp.int32, (n, r), 0)
    valid = row < n_valid
    g = jnp.where(valid, gate_t, -jnp.inf)
    rank = jnp.zeros((n, r), F32)
    for j in range(n):
        gj = g[j:j + 1, :]
        tie = jnp.where(gj == g, jnp.where(row > j, 1.0, 0.0), 0.0)
        rank = rank + jnp.where(gj > g, 1.0, tie)
    return jnp.where(valid, jnp.where(rank < n_sel, 1.0, 0.0), 0.0)


LOG2E = 1.4426950408889634


def _moba_prompt_kernel(q_ref, k_ref, v_ref, o_ref, kx_ref, vt_ref, means_ref, qx_ref, m_ref, l_ref,
                        acc_ref, sc_ref, *, n_blocks, hd, hg):
    t = pl.program_id(2)
    blk = MOBA_BLOCK
    kw = 2 * hd
    assert n_blocks <= hd
    c_exp = (hd ** -0.5) * LOG2E

    @pl.when(t == 0)
    def _():
        lane = lax.broadcasted_iota(jnp.int32, (blk, hd), 1)
        for n in range(n_blocks):
            rows = slice(n * blk, (n + 1) * blk)
            means_ref[n:n + 1, :] = jnp.mean(k_ref[rows, :], axis=0, keepdims=True)
            block_id = jnp.where(lane == n, 1.0, 0.0).astype(BF16)
            for i in range(hg):
                kx_ref[rows, i * kw:i * kw + hd] = k_ref[rows, i * hd:(i + 1) * hd].astype(BF16)
                kx_ref[rows, i * kw + hd:(i + 1) * kw] = block_id
                vt_ref[i * hd:(i + 1) * hd, rows] = jnp.transpose(v_ref[rows, i * hd:(i + 1) * hd]).astype(BF16)
        qx_ref[...] = jnp.zeros_like(qx_ref)

    r0 = pl.multiple_of(t * blk, blk)
    ki = lax.broadcasted_iota(jnp.int32, (blk, blk), 0)
    qi = lax.broadcasted_iota(jnp.int32, (blk, blk), 1)

    heads = [slice(i * hd, (i + 1) * hd) for i in range(hg)]
    gates = [_dot_nt(means_ref[:, cs], q_ref[:, cs], precision=lax.Precision.HIGHEST) for cs in heads]
    for i, cs in enumerate(heads):
        qx_ref[i, 0:hd, :] = jnp.transpose(q_ref[:, cs] * c_exp).astype(BF16)
    own = [_dot(kx_ref[pl.ds(r0, blk), i * kw:i * kw + hd], qx_ref[i, 0:hd, :]) for i in range(hg)]
    for i, cs in enumerate(heads):
        sel_t = _top_rows(gates[i], t, min(MOBA_TOPK, n_blocks))
        qx_ref[i, hd:hd + n_blocks, :] = ((1.0 - sel_t) * NEG_BIG).astype(BF16)
        s = jnp.where(ki <= qi, own[i], NEG_BIG)
        m0 = jnp.max(s, axis=0, keepdims=True)
        p = jnp.exp2(s - m0)
        m_ref[i] = m0
        l_ref[i] = jnp.sum(p, axis=0, keepdims=True)
        acc_ref[i] = _dot(vt_ref[cs, pl.ds(r0, blk)], p.astype(BF16))

    for i in range(hg):
        sc_ref[i] = _dot(kx_ref[0:blk, i * kw:(i + 1) * kw], qx_ref[i])

    def body(n, c):
        rn = pl.multiple_of(n * blk, blk)
        rx = pl.multiple_of(jnp.minimum(n + 1, t - 1) * blk, blk)
        ahead = [_dot(kx_ref[pl.ds(rx, blk), i * kw:(i + 1) * kw], qx_ref[i]) for i in range(hg)]
        upd = []
        for i, cs in enumerate(heads):
            s = sc_ref[i]
            m_new = jnp.maximum(m_ref[i], jnp.max(s, axis=0, keepdims=True))
            alpha = jnp.exp2(m_ref[i] - m_new)
            p = jnp.exp2(s - m_new)
            pv = _dot(vt_ref[cs, pl.ds(rn, blk)], p.astype(BF16))
            upd.append((m_new, alpha, jnp.sum(p, axis=0, keepdims=True), pv))
        for i, (m_new, alpha, psum, pv) in enumerate(upd):
            m_ref[i] = m_new
            l_ref[i] = alpha * l_ref[i] + psum
            acc_ref[i] = alpha * acc_ref[i] + pv
            sc_ref[i] = ahead[i]
        return c

    lax.fori_loop(0, t, body, 0)
    for i in range(hg):
        o_ref[:, i * hd:(i + 1) * hd] = jnp.transpose(acc_ref[i] / l_ref[i]).astype(o_ref.dtype)


def moba_prompt(q, k, v, *, batch, seq, heads, hd):
    n_blocks = seq // MOBA_BLOCK
    assert seq % MOBA_BLOCK == 0
    hg = 4 if heads % 4 == 0 else 1
    qspec = pl.BlockSpec((MOBA_BLOCK, hg * hd), lambda b, h, t: (b * n_blocks + t, h))
    kvspec = pl.BlockSpec((seq, hg * hd), lambda b, h, t: (b, h))
    return pl.pallas_call(
        functools.partial(_moba_prompt_kernel, n_blocks=n_blocks, hd=hd, hg=hg),
        grid=(batch, heads // hg, n_blocks),
        in_specs=[qspec, kvspec, kvspec],
        out_specs=qspec,
        out_shape=jax.ShapeDtypeStruct((batch * seq, heads * hd), BF16),
        scratch_shapes=[pltpu.VMEM((seq, hg * 2 * hd), BF16), pltpu.VMEM((hg * hd, seq), BF16),
                        pltpu.VMEM((n_blocks, hg * hd), F32),
                        pltpu.VMEM((hg, 2 * hd, MOBA_BLOCK), BF16),
                        pltpu.VMEM((hg, 1, MOBA_BLOCK), F32), pltpu.VMEM((hg, 1, MOBA_BLOCK), F32),
                        pltpu.VMEM((hg, hd, MOBA_BLOCK), F32),
                        pltpu.VMEM((hg, MOBA_BLOCK, MOBA_BLOCK), F32)],
        compiler_params=_cparams("arbitrary", "arbitrary", "arbitrary"),
        name="moba_prompt",
    )(q, k, v)


SAMPLE_ROWS = 8
MEANS_PAGES = 4


def _page_means_kernel(pt_ref, *refs, pages_per_block):
    kp = refs[:MEANS_PAGES]
    o_ref = refs[MEANS_PAGES]
    g = pl.program_id(1)
    for i in range(MEANS_PAGES // pages_per_block):
        tot = jnp.sum(kp[i * pages_per_block][...], axis=0)
        for j in range(1, pages_per_block):
            tot = tot + jnp.sum(kp[i * pages_per_block + j][...], axis=0)
        o_ref[0, g * (MEANS_PAGES // pages_per_block) + i] = tot * (1.0 / MOBA_BLOCK)


def page_block_means(cache_k, pt_flat, *, batch, n_pages):
    _, _, page, heads, hd = cache_k.shape
    ppb = MOBA_BLOCK // page
    nb = n_pages // ppb
    assert MEANS_PAGES % ppb == 0 and n_pages % MEANS_PAGES == 0

    def pspec(i):
        return pl.BlockSpec((None, None, page, heads, hd),
                            lambda b, g, pt: (0, pt[b * n_pages + g * MEANS_PAGES + i], 0, 0, 0))

    return pl.pallas_call(
        functools.partial(_page_means_kernel, pages_per_block=ppb),
        grid_spec=pltpu.PrefetchScalarGridSpec(
            num_scalar_prefetch=1,
            grid=(batch, n_pages // MEANS_PAGES),
            in_specs=[pspec(i) for i in range(MEANS_PAGES)],
            out_specs=pl.BlockSpec((1, nb, heads, hd), lambda b, g, pt: (b, 0, 0, 0)),
        ),
        out_shape=jax.ShapeDtypeStruct((batch, nb, heads, hd), F32),
        compiler_params=_cparams("arbitrary", "arbitrary"),
        name="page_means",
    )(pt_flat, *([cache_k] * MEANS_PAGES))


def _sample_gate_kernel(q_ref, m_ref, idx_ref, *, n_tok, n_sel):
    means = m_ref[0]
    nb = means.shape[0]
    blk_id = lax.broadcasted_iota(jnp.int32, (nb,) + means.shape[1:2] + (1,), 0)
    for t in range(n_tok):
        g = jnp.sum(means * q_ref[0, t][None], axis=-1, keepdims=True)
        for s in range(n_sel):
            m = jnp.max(g, axis=0, keepdims=True)
            idx = jnp.min(jnp.where(g == m, blk_id, nb), axis=0, keepdims=True)
            idx_ref[0, t * n_sel + s] = idx[0]
            g = jnp.where(blk_id == idx, -jnp.inf, g)


def sample_gate(q4, means, *, n_tok):
    batch, nb, heads, hd = means.shape
    assert nb >= MOBA_TOPK
    return pl.pallas_call(
        functools.partial(_sample_gate_kernel, n_tok=n_tok, n_sel=MOBA_TOPK),
        grid=(batch,),
        in_specs=[pl.BlockSpec((1, n_tok, heads, hd), lambda b: (b, 0, 0, 0)),
                  pl.BlockSpec((1, nb, heads, hd), lambda b: (b, 0, 0, 0))],
        out_specs=pl.BlockSpec((1, n_tok * MOBA_TOPK, heads, 1), lambda b: (b, 0, 0, 0)),
        out_shape=jax.ShapeDtypeStruct((batch, n_tok * MOBA_TOPK, heads, 1), jnp.int32),
        compiler_params=_cparams("arbitrary"),
        name="sample_gate",
    )(q4, means)


def _sample_attn_kernel(pp_ref, q_ref, kn_ref, vn_ref, ck_hbm, cv_hbm, o_ref, kbuf, vbuf, sem,
                        *, n_steps, heads, n_tok, n_pg, hd):
    step = pl.program_id(0)
    per_step = n_tok * n_pg

    def copies(st, slot):
        h = st % heads
        out = []
        for i in range(per_step):
            pg = pp_ref[st * per_step + i]
            out.append(pltpu.make_async_copy(ck_hbm.at[0, pg, :, h, :], kbuf.at[slot, i], sem.at[0, slot]))
            out.append(pltpu.make_async_copy(cv_hbm.at[0, pg, :, h, :], vbuf.at[slot, i], sem.at[1, slot]))
        return out

    slot = step % 2

    @pl.when(step == 0)
    def _():
        for cp in copies(step, 0):
            cp.start()

    @pl.when(step + 1 < n_steps)
    def _():
        for cp in copies(step + 1, 1 - slot):
            cp.start()

    for cp in copies(step, slot):
        cp.wait()

    r = SAMPLE_ROWS
    scale = hd ** -0.5
    kn = kn_ref[...]
    vn = vn_ref[...]
    row = lax.broadcasted_iota(jnp.int32, (r, 1), 0)
    o_ref[...] = jnp.zeros_like(o_ref)
    for t in range(n_tok):
        q_t = q_ref[t:t + 1, :]
        s_own = jnp.sum(kn * q_t, axis=-1, keepdims=True) * scale
        s_own = jnp.where(row <= t, s_own, NEG_BIG)
        s_pg = [jnp.sum(kbuf[slot, t * n_pg + j] * q_t, axis=-1, keepdims=True) * scale for j in range(n_pg)]
        m = jnp.max(s_own, axis=0, keepdims=True)
        for s in s_pg:
            m = jnp.maximum(m, jnp.max(s, axis=0, keepdims=True))
        p_own = jnp.exp(s_own - m)
        l = jnp.sum(p_own, axis=0, keepdims=True)
        acc = jnp.sum(p_own * vn, axis=0, keepdims=True)
        for j in range(n_pg):
            p = jnp.exp(s_pg[j] - m)
            l = l + jnp.sum(p, axis=0, keepdims=True)
            acc = acc + jnp.sum(p * vbuf[slot, t * n_pg + j], axis=0, keepdims=True)
        o_ref[t:t + 1, :] = acc / l


def sample_attention(q8, kn8, vn8, cache_k, cache_v, pp_flat, *, batch, heads, hd, n_tok, page):
    ppb = MOBA_BLOCK // page
    n_pg = MOBA_TOPK * ppb
    n_steps = batch * heads
    small = pl.BlockSpec((SAMPLE_ROWS, hd), lambda s, pp: (s // heads, s % heads))
    return pl.pallas_call(
        functools.partial(_sample_attn_kernel, n_steps=n_steps, heads=heads, n_tok=n_tok, n_pg=n_pg, hd=hd),
        grid_spec=pltpu.PrefetchScalarGridSpec(
            num_scalar_prefetch=1,
            grid=(n_steps,),
            in_specs=[small, small, small, pl.BlockSpec(memory_space=pl.ANY), pl.BlockSpec(memory_space=pl.ANY)],
            out_specs=small,
            scratch_shapes=[pltpu.VMEM((2, n_tok * n_pg, page, hd), F32),
                            pltpu.VMEM((2, n_tok * n_pg, page, hd), F32),
                            pltpu.SemaphoreType.DMA((2, 2))],
        ),
        out_shape=jax.ShapeDtypeStruct((batch * SAMPLE_ROWS, heads * hd), F32),
        compiler_params=_cparams("arbitrary"),
        name="sample_attn",
    )(pp_flat, q8, kn8, vn8, cache_k, cache_v)


def _norm_router_kernel(x_ref, w_ref, wr_ref, h_ref, info_ref, *, n_experts):
    x = x_ref[...]
    y = x * lax.rsqrt(jnp.mean(x * x, axis=-1, keepdims=True) + RMS_EPS) * w_ref[...]
    h_ref[...] = y
    logits = jnp.dot(y, wr_ref[...], precision=lax.Precision.HIGHEST, preferred_element_type=F32)
    r, n = logits.shape
    col = lax.broadcasted_iota(jnp.int32, (r, n), 1)
    lg = jnp.where(col < n_experts, logits, -jnp.inf)
    m1 = jnp.max(lg, axis=-1, keepdims=True)
    i1 = jnp.min(jnp.where(lg == m1, col, n), axis=-1, keepdims=True)
    lg2 = jnp.where(col == i1, -jnp.inf, lg)
    m2 = jnp.max(lg2, axis=-1, keepdims=True)
    i2 = jnp.min(jnp.where(lg2 == m2, col, n), axis=-1, keepdims=True)
    e2 = jnp.exp(m2 - m1)
    den = 1.0 + e2
    w1 = 1.0 / den
    w2 = e2 / den
    info = jnp.where(col == 0, i1.astype(F32),
                     jnp.where(col == 1, i2.astype(F32),
                               jnp.where(col == 2, w1, jnp.where(col == 3, w2, 0.0))))
    info_ref[...] = info


def norm_router(x, w, w_router):
    m, d = x.shape
    e = w_router.shape[1]
    tm = _pick(m, (256, 128, 64, 32, 16, 8))
    wr = jnp.zeros((d, 128), F32).at[:, :e].set(w_router)
    return pl.pallas_call(
        functools.partial(_norm_router_kernel, n_experts=e),
        grid=(m // tm,),
        in_specs=[pl.BlockSpec((tm, d), lambda i: (i, 0)), pl.BlockSpec((1, d), lambda i: (0, 0)),
                  pl.BlockSpec((d, 128), lambda i: (0, 0))],
        out_specs=[pl.BlockSpec((tm, d), lambda i: (i, 0)), pl.BlockSpec((tm, 128), lambda i: (i, 0))],
        out_shape=[jax.ShapeDtypeStruct((m, d), F32), jax.ShapeDtypeStruct((m, 128), F32)],
        compiler_params=_cparams("parallel"),
        name="norm_router",
    )(x, w.reshape(1, d), wr)


GATHER_TILE = 256


def _start_rows(src_hbm, dst_ref, sem, row_of):
    def start(r, c):
        pltpu.make_async_copy(src_hbm.at[pl.ds(row_of(r), 1), :], dst_ref.at[pl.ds(r, 1), :], sem).start()
        return c

    lax.fori_loop(0, dst_ref.shape[0], start, 0, unroll=8)


def _wait_rows(dst_ref, sem):
    pltpu.make_async_copy(dst_ref, dst_ref, sem).wait()


def _gather_cast_kernel(src_ref, ntile_ref, h_hbm, o_ref, buf_ref, sem):
    i = pl.program_id(0)
    n = ntile_ref[0]

    def issue(tile, slot):
        base = tile * GATHER_TILE
        _start_rows(h_hbm, buf_ref.at[slot], sem.at[slot], lambda r: src_ref[base + r])

    @pl.when(jnp.logical_and(i == 0, n > 0))
    def _():
        issue(0, 0)

    @pl.when(i + 1 < n)
    def _():
        issue(i + 1, (i + 1) % 2)

    @pl.when(i < n)
    def _():
        _wait_rows(buf_ref.at[i % 2], sem.at[i % 2])
        o_ref[...] = buf_ref[i % 2].astype(o_ref.dtype)

    @pl.when(i >= n)
    def _():
        o_ref[...] = jnp.zeros_like(o_ref)


def gather_cast(h, src_rows, n_tiles_used):
    s_rows = src_rows.shape[0]
    d = h.shape[1]
    return pl.pallas_call(
        _gather_cast_kernel,
        grid_spec=pltpu.PrefetchScalarGridSpec(
            num_scalar_prefetch=2,
            grid=(s_rows // GATHER_TILE,),
            in_specs=[pl.BlockSpec(memory_space=pl.ANY)],
            out_specs=pl.BlockSpec((GATHER_TILE, d), lambda i, s, n: (i, 0)),
            scratch_shapes=[pltpu.VMEM((2, GATHER_TILE, d), F32), pltpu.SemaphoreType.DMA((2,))],
        ),
        out_shape=jax.ShapeDtypeStruct((s_rows, d), BF16),
        compiler_params=_cparams("arbitrary"),
        name="gather_cast",
    )(src_rows, n_tiles_used, h)


EXPERT_SUB = 128
EXPERT_MAIN = 8
SEGMENT_ROWS = 2304


def _experts_kernel(seg_e_ref, seg_rows_ref, nseg_ref, x_ref, wg_ref, wu_ref, wd_ref, o_ref,
                    wgb_ref, wub_ref, wdb_ref):
    s = pl.program_id(0)
    f = pl.program_id(1)

    @pl.when(f == 0)
    def _():
        o_ref[...] = jnp.zeros_like(o_ref)

    def tiles(starts, size):
        gu = []
        for r0 in starts:
            xs = x_ref[pl.ds(r0, size), :]
            gu.append((_dot(xs, wgb_ref[...]), _dot(xs, wub_ref[...])))
        for r0, (g, u) in zip(starts, gu):
            a = (_silu(g) * u).astype(BF16)
            o_ref[pl.ds(r0, size), :] += _dot(a, wdb_ref[...])

    @pl.when(s < nseg_ref[0])
    def _():
        wgb_ref[...] = wg_ref[...].astype(BF16)
        wub_ref[...] = wu_ref[...].astype(BF16)
        wdb_ref[...] = wd_ref[...].astype(BF16)
        sub = EXPERT_SUB
        n_sub = (seg_rows_ref[s] + sub - 1) // sub
        n_main = n_sub // EXPERT_MAIN

        def body(i, c):
            r0 = pl.multiple_of(i * (EXPERT_MAIN * sub), EXPERT_MAIN * sub)
            half = EXPERT_MAIN // 2 * sub
            tiles([r0, r0 + half], half)
            return c

        lax.fori_loop(0, n_main, body, 0)
        done = n_main * EXPERT_MAIN
        size = EXPERT_MAIN // 2
        while size >= 1:
            take = ((n_sub - done) // size) > 0

            @pl.when(take)
            def _(done=done, size=size):
                tiles([pl.multiple_of(done * sub, sub)], size * sub)

            done = done + jnp.where(take, size, 0)
            size //= 2


def experts(x_sorted, wg, wu, wd, layer, seg_e, seg_rows, nseg, *, seg_cap):
    s_rows, d = x_sorted.shape
    s_max = s_rows // seg_cap
    f_dim = wg.shape[3]
    tf = _pick(f_dim, (256, 128))
    nf = f_dim // tf

    def seg_blk(s, n):
        return jnp.minimum(s, n[0] - 1)

    def fblk(s, f, n):
        return jnp.where(s < n[0], f, nf - 1)

    one = pl.Buffered(1)
    in_specs = [
        pl.BlockSpec((seg_cap, d), lambda s, f, e, r, n: (seg_blk(s, n), 0), pipeline_mode=one),
        pl.BlockSpec((None, None, d, tf), lambda s, f, e, r, n: (layer, e[s], 0, fblk(s, f, n))),
        pl.BlockSpec((None, None, d, tf), lambda s, f, e, r, n: (layer, e[s], 0, fblk(s, f, n))),
        pl.BlockSpec((None, None, tf, d), lambda s, f, e, r, n: (layer, e[s], fblk(s, f, n), 0)),
    ]
    return pl.pallas_call(
        _experts_kernel,
        grid_spec=pltpu.PrefetchScalarGridSpec(
            num_scalar_prefetch=3,
            grid=(s_max, nf),
            in_specs=in_specs,
            out_specs=pl.BlockSpec((seg_cap, d), lambda s, f, e, r, n: (s, 0), pipeline_mode=one),
            scratch_shapes=[pltpu.VMEM((d, tf), BF16), pltpu.VMEM((d, tf), BF16),
                            pltpu.VMEM((tf, d), BF16)],
        ),
        out_shape=jax.ShapeDtypeStruct((s_rows, d), F32),
        compiler_params=_cparams("arbitrary", "arbitrary"),
        name="experts",
    )(seg_e, seg_rows, nseg, x_sorted, wg, wu, wd)


def _combine_norm_kernel(s1_ref, s2_ref, ys_hbm, x_ref, info_ref, w_ref, o_ref, a_ref, b_ref, sem,
                         *, n_steps):
    i = pl.program_id(0)
    tm = x_ref.shape[0]

    def issue(tile, slot):
        base = tile * tm
        _start_rows(ys_hbm, a_ref.at[slot], sem.at[0, slot], lambda r: s1_ref[base + r])
        _start_rows(ys_hbm, b_ref.at[slot], sem.at[1, slot], lambda r: s2_ref[base + r])

    @pl.when(i == 0)
    def _():
        issue(0, 0)

    @pl.when(i + 1 < n_steps)
    def _():
        issue(i + 1, (i + 1) % 2)

    slot = i % 2
    _wait_rows(a_ref.at[slot], sem.at[0, slot])
    _wait_rows(b_ref.at[slot], sem.at[1, slot])
    info = info_ref[...]
    y = info[:, 2:3] * a_ref[slot] + info[:, 3:4] * b_ref[slot]
    x = x_ref[...] + y
    xn = x * lax.rsqrt(jnp.mean(x * x, axis=-1, keepdims=True) + RMS_EPS)
    o_ref[...] = xn * w_ref[...]


def combine_norm(x, y_sorted, slot1, slot2, info, w):
    m, d = x.shape
    tm = _pick(m, (256, 128, 64, 32, 16, 8))
    return pl.pallas_call(
        functools.partial(_combine_norm_kernel, n_steps=m // tm),
        grid_spec=pltpu.PrefetchScalarGridSpec(
            num_scalar_prefetch=2,
            grid=(m // tm,),
            in_specs=[pl.BlockSpec(memory_space=pl.ANY),
                      pl.BlockSpec((tm, d), lambda i, a, b: (i, 0)),
                      pl.BlockSpec((tm, 128), lambda i, a, b: (i, 0)),
                      pl.BlockSpec((1, d), lambda i, a, b: (0, 0))],
            out_specs=pl.BlockSpec((tm, d), lambda i, a, b: (i, 0)),
            scratch_shapes=[pltpu.VMEM((2, tm, d), F32), pltpu.VMEM((2, tm, d), F32),
                            pltpu.SemaphoreType.DMA((2, 2))],
        ),
        out_shape=jax.ShapeDtypeStruct((m, d), F32),
        compiler_params=_cparams("arbitrary"),
        name="combine_norm",
    )(slot1, slot2, y_sorted, x, info, w.reshape(1, d))


def _route_tables(info, n_real, n_experts, seg_cap):
    m = info.shape[0]
    e = info[:, :MOE_TOPK].astype(jnp.int32)
    valid = (jnp.arange(m) < n_real)[:, None]
    onehot = ((e[:, :, None] == jnp.arange(n_experts)[None, None, :]) & valid[:, :, None])
    onehot = onehot.reshape(m * MOE_TOPK, n_experts).astype(jnp.int32)
    rank = jnp.sum((jnp.cumsum(onehot, axis=0) - onehot) * onehot, axis=1)
    counts = jnp.sum(onehot, axis=0)
    nseg_e = (counts + seg_cap - 1) // seg_cap
    seg_start = jnp.cumsum(nseg_e) - nseg_e
    nseg = jnp.sum(nseg_e)
    s_max = (m * MOE_TOPK) // seg_cap + n_experts
    flat_e = e.reshape(-1)
    flat_valid = jnp.broadcast_to(valid, (m, MOE_TOPK)).reshape(-1)
    slot = (seg_start[flat_e] + rank // seg_cap) * seg_cap + rank % seg_cap
    slot = jnp.where(flat_valid, slot, 0)
    sid = jnp.arange(s_max)
    owner = jnp.argmax((sid[:, None] >= seg_start[None, :]) & (sid[:, None] < (seg_start + nseg_e)[None, :]), axis=1)
    seg_e = jnp.where(sid < nseg, owner, owner[jnp.maximum(nseg - 1, 0)]).astype(jnp.int32)
    within = sid - seg_start[seg_e]
    seg_rows = jnp.clip(counts[seg_e] - within * seg_cap, 0, seg_cap)
    seg_rows = jnp.where(sid < nseg, seg_rows, 0).astype(jnp.int32)
    tok = jnp.repeat(jnp.arange(m, dtype=jnp.int32), MOE_TOPK)
    src = jnp.full((s_max * seg_cap,), m - 1, jnp.int32)
    src = src.at[jnp.where(flat_valid, slot, s_max * seg_cap)].set(tok, mode="drop")
    n_tiles_used = (nseg * (seg_cap // GATHER_TILE)).astype(jnp.int32).reshape(1)
    slots = slot.reshape(m, MOE_TOPK).astype(jnp.int32)
    return src, n_tiles_used, seg_e, seg_rows, nseg.astype(jnp.int32).reshape(1), slots[:, 0], slots[:, 1]


def kernel(x_prompt, x_sample, state_ret, cache_k, cache_v, page_table, norm_w, final_norm_w,
           ret_w_in, ret_w_out, moba_w_qkv, moba_w_out, ffn_w_gate, ffn_w_up, ffn_w_down,
           moe_w_router, moe_w_gate, moe_w_up, moe_w_down):
    bp, sp, d = x_prompt.shape
    bs, ts, _ = x_sample.shape
    _, _, r_heads, dk, dv = state_ret.shape
    _, n_pool, page, m_heads, hd = cache_k.shape
    n_pages = page_table.shape[1]
    past_len = n_pages * page
    n_experts = moe_w_router.shape[2]
    assert sp % RET_CHUNK == 0 and ts < RET_CHUNK and ts <= SAMPLE_ROWS
    assert MOBA_BLOCK % page == 0 and past_len % MOBA_BLOCK == 0 and sp % MOBA_BLOCK == 0

    n_p = bp * sp
    n_s = bs * ts
    n_real = n_p + n_s
    xp = x_prompt.reshape(n_p, d)
    xs = x_sample.reshape(n_s, d)

    hp = rmsnorm(xp, norm_w[0, 0], BF16)
    hs = rmsnorm(xs, norm_w[0, 0], F32)
    n_qk = 2 * r_heads * dk
    qk = matmul(hp, ret_w_in, 0, n_off=0, n=n_qk, name="ret_qk_proj")
    vg = matmul(hp, ret_w_in, 0, n_off=n_qk, n=2 * r_heads * dv, out_dtype=BF16, name="ret_vg_proj")
    pos_p = jnp.arange(sp, dtype=jnp.int32)
    zero_state = jnp.zeros((bp, r_heads, dk, dv), F32)
    y_p, st_p = retention(qk, vg, zero_state, pos_p, batch=bp, n_chunks=sp // RET_CHUNK,
                          chunk=RET_CHUNK, valid=RET_CHUNK, heads=r_heads, dk=dk, dv=dv)
    qkvg_s = matmul_precise(hs, ret_w_in, 0, name="ret_in_proj_s")

    def pad_chunk(a2):
        a3 = jnp.pad(a2.reshape(bs, ts, -1), ((0, 0), (0, RET_CHUNK - ts), (0, 0)))
        return a3.reshape(bs * RET_CHUNK, -1)

    pos_s = past_len + jnp.arange(RET_CHUNK, dtype=jnp.int32)
    y_s, st_s = retention(pad_chunk(qkvg_s[:, :n_qk]), pad_chunk(qkvg_s[:, n_qk:]), state_ret[0], pos_s,
                          batch=bs, n_chunks=1, chunk=RET_CHUNK, valid=ts, heads=r_heads, dk=dk, dv=dv,
                          precise=True)
    y_s = y_s.reshape(bs, RET_CHUNK, -1)[:, :ts].reshape(n_s, -1)
    xp = matmul(y_p, ret_w_out, 0, res=xp, name="ret_out_proj")
    xs = matmul_precise(y_s, ret_w_out, 0, res=xs, name="ret_out_proj_s")

    hp = rmsnorm(xp, norm_w[0, 1], BF16)
    hs = rmsnorm(xs, norm_w[0, 1], F32)
    xp = matmul(gate_up(hp, ffn_w_gate, ffn_w_up, 0), ffn_w_down, 0, res=xp, name="ffn_down")
    xs = matmul_precise(gate_up_precise(hs, ffn_w_gate, ffn_w_up, 0), ffn_w_down, 0, res=xs, name="ffn_down_s")

    hp = rmsnorm(xp, norm_w[1, 0], BF16)
    hs = rmsnorm(xs, norm_w[1, 0], F32)
    q = matmul(hp, moba_w_qkv, 0, n_off=0, n=d, name="moba_q_proj")
    k = matmul(hp, moba_w_qkv, 0, n_off=d, n=d, name="moba_k_proj")
    v = matmul(hp, moba_w_qkv, 0, n_off=2 * d, n=d, name="moba_v_proj")
    o_p = moba_prompt(q, k, v, batch=bp, seq=sp, heads=m_heads, hd=hd)
    qkv_s = matmul_precise(hs, moba_w_qkv, 0, name="moba_qkv_proj_s")
    q_s, k_s, v_s = qkv_s[:, :d], qkv_s[:, d:2 * d], qkv_s[:, 2 * d:]

    def pad8(a2):
        return jnp.pad(a2.reshape(bs, ts, d), ((0, 0), (0, SAMPLE_ROWS - ts), (0, 0))).reshape(bs * SAMPLE_ROWS, d)

    means = page_block_means(cache_k, page_table.reshape(-1), batch=bs, n_pages=n_pages)
    sel = sample_gate(q_s.reshape(bs, ts, m_heads, hd), means, n_tok=ts)
    sel = sel.reshape(bs, ts, MOBA_TOPK, m_heads).transpose(0, 3, 1, 2)
    ppb = MOBA_BLOCK // page
    lpage = sel[..., None] * ppb + jnp.arange(ppb, dtype=jnp.int32)
    pp = page_table[jnp.arange(bs)[:, None, None, None], lpage.reshape(bs, m_heads, ts, MOBA_TOPK * ppb)]
    o_s8 = sample_attention(pad8(q_s), pad8(k_s), pad8(v_s), cache_k, cache_v, pp.reshape(-1).astype(jnp.int32),
                            batch=bs, heads=m_heads, hd=hd, n_tok=ts, page=page)
    o_s = o_s8.reshape(bs, SAMPLE_ROWS, d)[:, :ts].reshape(n_s, d)
    xp = matmul(o_p, moba_w_out, 0, res=xp, name="moba_out_proj")
    xs = matmul_precise(o_s, moba_w_out, 0, res=xs, name="moba_out_proj_s")

    m = -(-(n_real + 1) // ROW_ALIGN) * ROW_ALIGN
    x = jnp.concatenate([xp, xs, jnp.zeros((m - n_real, d), F32)], axis=0)
    hf, info = norm_router(x, norm_w[1, 1], moe_w_router[0])
    seg_cap = SEGMENT_ROWS if m * MOE_TOPK >= 4 * SEGMENT_ROWS else 2 * GATHER_TILE
    src, n_tiles_used, seg_e, seg_rows, nseg, slot1, slot2 = _route_tables(info, n_real, n_experts, seg_cap)
    x_sorted = gather_cast(hf, src, n_tiles_used)
    y_sorted = experts(x_sorted, moe_w_gate, moe_w_up, moe_w_down, 0, seg_e, seg_rows, nseg, seg_cap=seg_cap)
    y = combine_norm(x, y_sorted, slot1, slot2, info, final_norm_w)

    y_prompt = y[:n_p].reshape(bp, sp, d)
    y_sample = y[n_p:n_real].reshape(bs, ts, d)
    kv_shape_p = (1, bp, sp, m_heads, hd)
    kv_shape_s = (1, bs, ts, m_heads, hd)
    return (y_prompt, y_sample, st_p[None], st_s[None],
            k.reshape(kv_shape_p), v.reshape(kv_shape_p), k_s.reshape(kv_shape_s), v_s.reshape(kv_shape_s))
```

```python
import functools

import jax
import jax.numpy as jnp
from jax import lax
from jax.experimental import pallas as pl
from jax.experimental.pallas import tpu as pltpu

RET_CHUNK = 128
ROPE_BASE = 10000.0
MOBA_BLOCK = 256
MOBA_TOPK = 3
MOE_TOPK = 2
RMS_EPS = 1e-6
GN_EPS = 1e-5

V7X_VMEM_LIMIT_BYTES = 56 * 1024 * 1024
ROW_ALIGN = 256
NEG_BIG = -1e30

F32 = jnp.float32
BF16 = jnp.bfloat16


def _cparams(*sem):
    return pltpu.CompilerParams(dimension_semantics=sem, vmem_limit_bytes=V7X_VMEM_LIMIT_BYTES)


def _pick(n, prefs):
    for p in prefs:
        if p <= n and n % p == 0:
            return p
    return n


def _silu(g):
    return g * (1.0 / (1.0 + jnp.exp(-g)))


def _dot(a, b):
    return jnp.dot(a, b, preferred_element_type=F32)


def _dot_nt(a, b, precision=None):
    return lax.dot_general(a, b, (((1,), (1,)), ((), ())), precision=precision,
                           preferred_element_type=F32)


def _split(x):
    hi = x.astype(BF16)
    return hi, (x - hi.astype(F32)).astype(BF16)


def _dot3(a, b):
    ah, al = _split(a)
    bh, bl = _split(b)
    return _dot(ah, bh) + (_dot(al, bh) + _dot(ah, bl))


def _dot3_nt(a, b):
    ah, al = _split(a)
    bh, bl = _split(b)
    return _dot_nt(ah, bh) + (_dot_nt(al, bh) + _dot_nt(ah, bl))


def _dot1(a, b):
    return _dot(a.astype(BF16), b.astype(BF16))


def _dot1_nt(a, b):
    return _dot_nt(a.astype(BF16), b.astype(BF16))


def _rmsnorm_kernel(x_ref, w_ref, o_ref):
    x = x_ref[...]
    y = x * lax.rsqrt(jnp.mean(x * x, axis=-1, keepdims=True) + RMS_EPS)
    o_ref[...] = (y * w_ref[...]).astype(o_ref.dtype)


def rmsnorm(x, w, out_dtype):
    m, d = x.shape
    tm = _pick(m, (256, 128, 64, 32, 16, 8))
    return pl.pallas_call(
        _rmsnorm_kernel,
        grid=(m // tm,),
        in_specs=[pl.BlockSpec((tm, d), lambda i: (i, 0)), pl.BlockSpec((1, d), lambda i: (0, 0))],
        out_specs=pl.BlockSpec((tm, d), lambda i: (i, 0)),
        out_shape=jax.ShapeDtypeStruct((m, d), out_dtype),
        compiler_params=_cparams("parallel"),
        name="rmsnorm",
    )(x, w.reshape(1, d))


def _mm_kernel(*refs, has_res):
    if has_res:
        x_ref, w_ref, r_ref, o_ref, wb_ref = refs
    else:
        x_ref, w_ref, o_ref, wb_ref = refs

    @pl.when(pl.program_id(1) == 0)
    def _():
        wb_ref[...] = w_ref[...].astype(BF16)

    acc = _dot(x_ref[...], wb_ref[...])
    if has_res:
        acc = acc + r_ref[...]
    o_ref[...] = acc.astype(o_ref.dtype)


TILE_VMEM_BUDGET = 44 * 1024 * 1024


def _matmul_tiles(m, k, n, n_w, out_bytes, has_res):
    best = None
    for tm in (2048, 1056, 1024, 528, 512, 256, 128, 64, 32, 16):
        if m % tm:
            continue
        for tn in (1024, 512, 256, 128):
            if n % tn:
                continue
            need = (2 * tm * k * 2 + n_w * (2 * k * tn * 4 + k * tn * 2)
                    + 2 * tm * tn * out_bytes + (2 * tm * tn * 4 if has_res else 0))
            score = (-(m // tm) * (n // tn), tn)
            if need <= TILE_VMEM_BUDGET and (best is None or score > best[0]):
                best = (score, (tm, tn))
    assert best is not None
    return best[1]


def matmul(x, w3, layer, *, name, n_off=0, n=None, res=None, out_dtype=F32):
    m, k = x.shape
    n = w3.shape[2] if n is None else n
    tm, tn = _matmul_tiles(m, k, n, 1, jnp.dtype(out_dtype).itemsize, res is not None)
    assert n_off % tn == 0
    off = n_off // tn
    in_specs = [pl.BlockSpec((tm, k), lambda j, i: (i, 0)),
                pl.BlockSpec((None, k, tn), lambda j, i: (layer, 0, j + off))]
    args = [x, w3]
    if res is not None:
        in_specs.append(pl.BlockSpec((tm, tn), lambda j, i: (i, j)))
        args.append(res)
    return pl.pallas_call(
        functools.partial(_mm_kernel, has_res=res is not None),
        grid=(n // tn, m // tm),
        in_specs=in_specs,
        out_specs=pl.BlockSpec((tm, tn), lambda j, i: (i, j)),
        out_shape=jax.ShapeDtypeStruct((m, n), out_dtype),
        scratch_shapes=[pltpu.VMEM((k, tn), BF16)],
        compiler_params=_cparams("arbitrary", "arbitrary"),
        name=name,
    )(*args)


def _gateup_kernel(x_ref, wg_ref, wu_ref, o_ref, wgb_ref, wub_ref):
    @pl.when(pl.program_id(1) == 0)
    def _():
        wgb_ref[...] = wg_ref[...].astype(BF16)
        wub_ref[...] = wu_ref[...].astype(BF16)

    x = x_ref[...]
    g = _dot(x, wgb_ref[...])
    u = _dot(x, wub_ref[...])
    o_ref[...] = (_silu(g) * u).astype(o_ref.dtype)


def gate_up(x, wg3, wu3, layer):
    m, k = x.shape
    f = wg3.shape[2]
    tm, tn = _matmul_tiles(m, k, f, 2, 2, False)
    wspec = pl.BlockSpec((None, k, tn), lambda j, i: (layer, 0, j))
    return pl.pallas_call(
        _gateup_kernel,
        grid=(f // tn, m // tm),
        in_specs=[pl.BlockSpec((tm, k), lambda j, i: (i, 0)), wspec, wspec],
        out_specs=pl.BlockSpec((tm, tn), lambda j, i: (i, j)),
        out_shape=jax.ShapeDtypeStruct((m, f), BF16),
        scratch_shapes=[pltpu.VMEM((k, tn), BF16), pltpu.VMEM((k, tn), BF16)],
        compiler_params=_cparams("arbitrary", "arbitrary"),
        name="gate_up",
    )(x, wg3, wu3)


PRECISE_W_TILE_BYTES = 8 * 1024 * 1024


def _precise_tn(k, n, n_w=1):
    fits = [tn for tn in range(128, n + 1, 128) if n % tn == 0 and n_w * k * tn * 4 <= PRECISE_W_TILE_BYTES]
    assert fits
    return fits[-1]


def _mm3_kernel(*refs, has_res):
    if has_res:
        x_ref, w_ref, r_ref, o_ref = refs
    else:
        x_ref, w_ref, o_ref = refs
    acc = _dot3(x_ref[...], w_ref[...])
    if has_res:
        acc = acc + r_ref[...]
    o_ref[...] = acc


def matmul_precise(x, w3, layer, *, name, n_off=0, n=None, res=None):
    m, k = x.shape
    n = w3.shape[2] if n is None else n
    tn = _precise_tn(k, n)
    assert n_off % tn == 0
    off = n_off // tn
    in_specs = [pl.BlockSpec((m, k), lambda j: (0, 0)),
                pl.BlockSpec((None, k, tn), lambda j: (layer, 0, j + off))]
    args = [x, w3]
    if res is not None:
        in_specs.append(pl.BlockSpec((m, tn), lambda j: (0, j)))
        args.append(res)
    return pl.pallas_call(
        functools.partial(_mm3_kernel, has_res=res is not None),
        grid=(n // tn,),
        in_specs=in_specs,
        out_specs=pl.BlockSpec((m, tn), lambda j: (0, j)),
        out_shape=jax.ShapeDtypeStruct((m, n), F32),
        compiler_params=_cparams("arbitrary"),
        name=name,
    )(*args)


def _gateup3_kernel(x_ref, wg_ref, wu_ref, o_ref):
    x = x_ref[...]
    g = _dot3(x, wg_ref[...])
    u = _dot3(x, wu_ref[...])
    o_ref[...] = _silu(g) * u


def gate_up_precise(x, wg3, wu3, layer):
    m, k = x.shape
    f = wg3.shape[2]
    tn = _precise_tn(k, f, 2)
    wspec = pl.BlockSpec((None, k, tn), lambda j: (layer, 0, j))
    return pl.pallas_call(
        _gateup3_kernel,
        grid=(f // tn,),
        in_specs=[pl.BlockSpec((m, k), lambda j: (0, 0)), wspec, wspec],
        out_specs=pl.BlockSpec((m, tn), lambda j: (0, j)),
        out_shape=jax.ShapeDtypeStruct((m, f), F32),
        compiler_params=_cparams("arbitrary"),
        name="gate_up_precise",
    )(x, wg3, wu3)


def _retention_kernel(q_ref, k_ref, v_ref, g_ref, cos_ref, sin_ref, dec_ref, qd_ref, kd_ref,
                      cd_ref, s0_ref, y_ref, sout_ref, state_ref, *, n_chunks, dk, dv, hg, precise):
    c = pl.program_id(2)

    @pl.when(c == 0)
    def _():
        state_ref[...] = s0_ref[0]

    half = dk // 2
    cos = cos_ref[...]
    sin = sin_ref[...]

    def rot(x):
        x1, x2 = x[:, :half], x[:, half:]
        return jnp.concatenate([x1 * cos - x2 * sin, x1 * sin + x2 * cos], axis=-1)

    qs, ks, vs, sts = [], [], [], []
    for i in range(hg):
        qs.append(rot(q_ref[:, i * dk:(i + 1) * dk]))
        ks.append(rot(k_ref[:, i * dk:(i + 1) * dk]) * (dk ** -0.5))
        vs.append(v_ref[:, i * dv:(i + 1) * dv])
        sts.append(state_ref[i])
    mm, mm_nt = (_dot3, _dot3_nt) if precise else (_dot1, _dot1_nt)
    raw = [mm_nt(qs[i], ks[i]) for i in range(hg)]
    crosses = [mm(qs[i], sts[i]) for i in range(hg)]
    for i in range(hg):
        kdt = jnp.transpose(ks[i] * kd_ref[i])
        state_ref[i] = sts[i] * cd_ref[i] + mm(kdt, vs[i])
    inners = [mm(raw[i] * dec_ref[i], vs[i]) for i in range(hg)]
    for i in range(hg):
        y = inners[i] + crosses[i] * qd_ref[i]
        mu = jnp.mean(y, axis=-1, keepdims=True)
        d = y - mu
        var = jnp.mean(d * d, axis=-1, keepdims=True)
        yn = d * lax.rsqrt(var + GN_EPS)
        g = g_ref[:, i * dv:(i + 1) * dv].astype(F32)
        y_ref[:, i * dv:(i + 1) * dv] = (_silu(g) * yn).astype(y_ref.dtype)

    @pl.when(c == n_chunks - 1)
    def _():
        sout_ref[0] = state_ref[...]


def _retention_tables(heads, chunk, valid):
    lg = jnp.log1p(-jnp.exp2(-5.0 - jnp.arange(heads, dtype=F32)))
    i = jnp.arange(chunk, dtype=F32)
    ok = i < valid
    diff = i[:, None] - i[None, :]
    dec = jnp.where((diff[None] >= 0) & ok[None, :, None] & ok[None, None, :],
                    jnp.exp(jnp.maximum(diff, 0.0)[None] * lg[:, None, None]), 0.0)
    qd = jnp.where(ok[None, :], jnp.exp((i + 1.0)[None, :] * lg[:, None]), 0.0)
    kd = jnp.where(ok[None, :], jnp.exp((valid - 1.0 - i)[None, :] * lg[:, None]), 0.0)
    cd = jnp.exp(valid * lg)
    return dec, qd[:, :, None], kd[:, :, None], cd[:, None, None]


def _rope_tables(pos, half):
    inv = ROPE_BASE ** (-jnp.arange(half, dtype=F32) / half)
    ang = pos.astype(F32)[:, None] * inv[None, :]
    return jnp.cos(ang), jnp.sin(ang)


def retention(qk, vg, state0, pos, *, batch, n_chunks, chunk, valid, heads, dk, dv, precise=False):
    rows = batch * n_chunks * chunk
    assert qk.shape[0] >= rows and vg.shape[0] >= rows
    hg = 2 if heads % 2 == 0 else 1
    ng = heads // hg
    dec, qd, kd, cd = _retention_tables(heads, chunk, valid)
    cos, sin = _rope_tables(pos, dk // 2)
    row = lambda b, h, c: b * n_chunks + c
    in_specs = [
        pl.BlockSpec((chunk, hg * dk), lambda b, h, c: (row(b, h, c), h)),
        pl.BlockSpec((chunk, hg * dk), lambda b, h, c: (row(b, h, c), ng + h)),
        pl.BlockSpec((chunk, hg * dv), lambda b, h, c: (row(b, h, c), h)),
        pl.BlockSpec((chunk, hg * dv), lambda b, h, c: (row(b, h, c), ng + h)),
        pl.BlockSpec((chunk, dk // 2), lambda b, h, c: (c, 0)),
        pl.BlockSpec((chunk, dk // 2), lambda b, h, c: (c, 0)),
        pl.BlockSpec((hg, chunk, chunk), lambda b, h, c: (h, 0, 0)),
        pl.BlockSpec((hg, chunk, 1), lambda b, h, c: (h, 0, 0)),
        pl.BlockSpec((hg, chunk, 1), lambda b, h, c: (h, 0, 0)),
        pl.BlockSpec((hg, 1, 1), lambda b, h, c: (h, 0, 0)),
        pl.BlockSpec((1, hg, dk, dv), lambda b, h, c: (b, h, 0, 0)),
    ]
    out_specs = [
        pl.BlockSpec((chunk, hg * dv), lambda b, h, c: (row(b, h, c), h)),
        pl.BlockSpec((1, hg, dk, dv), lambda b, h, c: (b, h, 0, 0)),
    ]
    return pl.pallas_call(
        functools.partial(_retention_kernel, n_chunks=n_chunks, dk=dk, dv=dv, hg=hg, precise=precise),
        grid=(batch, ng, n_chunks),
        in_specs=in_specs,
        out_specs=out_specs,
        out_shape=[jax.ShapeDtypeStruct((rows, heads * dv), F32 if precise else BF16),
                   jax.ShapeDtypeStruct((batch, heads, dk, dv), F32)],
        scratch_shapes=[pltpu.VMEM((hg, dk, dv), F32)],
        compiler_params=_cparams("arbitrary", "arbitrary", "arbitrary"),
        name="retention",
    )(qk, qk, vg, vg, cos, sin, dec, qd, kd, cd, state0)


def _top_rows(gate_t, n_valid, n_sel):
    n, r = gate_t.shape
    row = lax.broadcasted_iota(jnp.int32, (n, r), 0)
    valid = row < n_valid
    g = jnp.where(valid, gate_t, -jnp.inf)
    rank = jnp.zeros((n, r), F32)
    for j in range(n):
        gj = g[j:j + 1, :]
        tie = jnp.where(gj == g, jnp.where(row > j, 1.0, 0.0), 0.0)
        rank = rank + jnp.where(gj > g, 1.0, tie)
    return jnp.where(valid, jnp.where(rank < n_sel, 1.0, 0.0), 0.0)


LOG2E = 1.4426950408889634


def _moba_prompt_kernel(q_ref, k_ref, v_ref, o_ref, kx_ref, vt_ref, means_ref, qx_ref, m_ref, l_ref,
                        acc_ref, sc_ref, *, n_blocks, hd, hg):
    t = pl.program_id(2)
    blk = MOBA_BLOCK
    kw = 2 * hd
    assert n_blocks <= hd
    c_exp = (hd ** -0.5) * LOG2E

    @pl.when(t == 0)
    def _():
        lane = lax.broadcasted_iota(jnp.int32, (blk, hd), 1)
        for n in range(n_blocks):
            rows = slice(n * blk, (n + 1) * blk)
            means_ref[n:n + 1, :] = jnp.mean(k_ref[rows, :], axis=0, keepdims=True)
            block_id = jnp.where(lane == n, 1.0, 0.0).astype(BF16)
            for i in range(hg):
                kx_ref[rows, i * kw:i * kw + hd] = k_ref[rows, i * hd:(i + 1) * hd].astype(BF16)
                kx_ref[rows, i * kw + hd:(i + 1) * kw] = block_id
                vt_ref[i * hd:(i + 1) * hd, rows] = jnp.transpose(v_ref[rows, i * hd:(i + 1) * hd]).astype(BF16)
        qx_ref[...] = jnp.zeros_like(qx_ref)

    r0 = pl.multiple_of(t * blk, blk)
    ki = lax.broadcasted_iota(jnp.int32, (blk, blk), 0)
    qi = lax.broadcasted_iota(jnp.int32, (blk, blk), 1)

    heads = [slice(i * hd, (i + 1) * hd) for i in range(hg)]
    gates = [_dot3_nt(means_ref[:, cs], q_ref[:, cs]) for cs in heads]
    for i, cs in enumerate(heads):
        qx_ref[i, 0:hd, :] = jnp.transpose(q_ref[:, cs] * c_exp).astype(BF16)
    own = [_dot(kx_ref[pl.ds(r0, blk), i * kw:i * kw + hd], qx_ref[i, 0:hd, :]) for i in range(hg)]
    for i, cs in enumerate(heads):
        sel_t = _top_rows(gates[i], t, min(MOBA_TOPK, n_blocks))
        qx_ref[i, hd:hd + n_blocks, :] = ((1.0 - sel_t) * NEG_BIG).astype(BF16)
        s = jnp.where(ki <= qi, own[i], NEG_BIG)
        m0 = jnp.max(s, axis=0, keepdims=True)
        p = jnp.exp2(s - m0)
        m_ref[i] = m0
        l_ref[i] = jnp.sum(p, axis=0, keepdims=True)
        acc_ref[i] = _dot(vt_ref[cs, pl.ds(r0, blk)], p.astype(BF16))

    for i in range(hg):
        sc_ref[i] = _dot(kx_ref[0:blk, i * kw:(i + 1) * kw], qx_ref[i])

    def body(n, c):
        rn = pl.multiple_of(n * blk, blk)
        rx = pl.multiple_of(jnp.minimum(n + 1, t - 1) * blk, blk)
        ahead = [_dot(kx_ref[pl.ds(rx, blk), i * kw:(i + 1) * kw], qx_ref[i]) for i in range(hg)]
        upd = []
        for i, cs in enumerate(heads):
            s = sc_ref[i]
            m_new = jnp.maximum(m_ref[i], jnp.max(s, axis=0, keepdims=True))
            alpha = jnp.exp2(m_ref[i] - m_new)
            p = jnp.exp2(s - m_new)
            pv = _dot(vt_ref[cs, pl.ds(rn, blk)], p.astype(BF16))
            upd.append((m_new, alpha, jnp.sum(p, axis=0, keepdims=True), pv))
        for i, (m_new, alpha, psum, pv) in enumerate(upd):
            m_ref[i] = m_new
            l_ref[i] = alpha * l_ref[i] + psum
            acc_ref[i] = alpha * acc_ref[i] + pv
            sc_ref[i] = ahead[i]
        return c

    lax.fori_loop(0, t, body, 0)
    for i in range(hg):
        o_ref[:, i * hd:(i + 1) * hd] = jnp.transpose(acc_ref[i] / l_ref[i]).astype(o_ref.dtype)


def moba_prompt(q, k, v, *, batch, seq, heads, hd):
    n_blocks = seq // MOBA_BLOCK
    assert seq % MOBA_BLOCK == 0
    hg = 4 if heads % 4 == 0 else 1
    qspec = pl.BlockSpec((MOBA_BLOCK, hg * hd), lambda b, h, t: (b * n_blocks + t, h))
    kvspec = pl.BlockSpec((seq, hg * hd), lambda b, h, t: (b, h))
    return pl.pallas_call(
        functools.partial(_moba_prompt_kernel, n_blocks=n_blocks, hd=hd, hg=hg),
        grid=(batch, heads // hg, n_blocks),
        in_specs=[qspec, kvspec, kvspec],
        out_specs=qspec,
        out_shape=jax.ShapeDtypeStruct((batch * seq, heads * hd), BF16),
        scratch_shapes=[pltpu.VMEM((seq, hg * 2 * hd), BF16), pltpu.VMEM((hg * hd, seq), BF16),
                        pltpu.VMEM((n_blocks, hg * hd), F32),
                        pltpu.VMEM((hg, 2 * hd, MOBA_BLOCK), BF16),
                        pltpu.VMEM((hg, 1, MOBA_BLOCK), F32), pltpu.VMEM((hg, 1, MOBA_BLOCK), F32),
                        pltpu.VMEM((hg, hd, MOBA_BLOCK), F32),
                        pltpu.VMEM((hg, MOBA_BLOCK, MOBA_BLOCK), F32)],
        compiler_params=_cparams("arbitrary", "arbitrary", "arbitrary"),
        name="moba_prompt",
    )(q, k, v)


SAMPLE_ROWS = 8
MEANS_PAGES = 4


def _page_means_kernel(pt_ref, *refs, pages_per_block):
    kp = refs[:MEANS_PAGES]
    o_ref = refs[MEANS_PAGES]
    g = pl.program_id(1)
    for i in range(MEANS_PAGES // pages_per_block):
        tot = jnp.sum(kp[i * pages_per_block][...], axis=0)
        for j in range(1, pages_per_block):
            tot = tot + jnp.sum(kp[i * pages_per_block + j][...], axis=0)
        o_ref[0, g * (MEANS_PAGES // pages_per_block) + i] = tot * (1.0 / MOBA_BLOCK)


def page_block_means(cache_k, pt_flat, *, batch, n_pages):
    _, _, page, heads, hd = cache_k.shape
    ppb = MOBA_BLOCK // page
    nb = n_pages // ppb
    assert MEANS_PAGES % ppb == 0 and n_pages % MEANS_PAGES == 0

    def pspec(i):
        return pl.BlockSpec((None, None, page, heads, hd),
                            lambda b, g, pt: (0, pt[b * n_pages + g * MEANS_PAGES + i], 0, 0, 0))

    return pl.pallas_call(
        functools.partial(_page_means_kernel, pages_per_block=ppb),
        grid_spec=pltpu.PrefetchScalarGridSpec(
            num_scalar_prefetch=1,
            grid=(batch, n_pages // MEANS_PAGES),
            in_specs=[pspec(i) for i in range(MEANS_PAGES)],
            out_specs=pl.BlockSpec((1, nb, heads, hd), lambda b, g, pt: (b, 0, 0, 0)),
        ),
        out_shape=jax.ShapeDtypeStruct((batch, nb, heads, hd), F32),
        compiler_params=_cparams("arbitrary", "arbitrary"),
        name="page_means",
    )(pt_flat, *([cache_k] * MEANS_PAGES))


def _sample_gate_kernel(q_ref, m_ref, idx_ref, *, n_tok, n_sel):
    means = m_ref[0]
    nb = means.shape[0]
    blk_id = lax.broadcasted_iota(jnp.int32, (nb,) + means.shape[1:2] + (1,), 0)
    for t in range(n_tok):
        g = jnp.sum(means * q_ref[0, t][None], axis=-1, keepdims=True)
        for s in range(n_sel):
            m = jnp.max(g, axis=0, keepdims=True)
            idx = jnp.min(jnp.where(g == m, blk_id, nb), axis=0, keepdims=True)
            idx_ref[0, t * n_sel + s] = idx[0]
            g = jnp.where(blk_id == idx, -jnp.inf, g)


def sample_gate(q4, means, *, n_tok):
    batch, nb, heads, hd = means.shape
    assert nb >= MOBA_TOPK
    return pl.pallas_call(
        functools.partial(_sample_gate_kernel, n_tok=n_tok, n_sel=MOBA_TOPK),
        grid=(batch,),
        in_specs=[pl.BlockSpec((1, n_tok, heads, hd), lambda b: (b, 0, 0, 0)),
                  pl.BlockSpec((1, nb, heads, hd), lambda b: (b, 0, 0, 0))],
        out_specs=pl.BlockSpec((1, n_tok * MOBA_TOPK, heads, 1), lambda b: (b, 0, 0, 0)),
        out_shape=jax.ShapeDtypeStruct((batch, n_tok * MOBA_TOPK, heads, 1), jnp.int32),
        compiler_params=_cparams("arbitrary"),
        name="sample_gate",
    )(q4, means)


def _sample_attn_kernel(pp_ref, q_ref, kn_ref, vn_ref, ck_hbm, cv_hbm, o_ref, kbuf, vbuf, sem,
                        *, n_steps, heads, n_tok, n_pg, hd):
    step = pl.program_id(0)
    per_step = n_tok * n_pg

    def copies(st, slot):
        h = st % heads
        out = []
        for i in range(per_step):
            pg = pp_ref[st * per_step + i]
            out.append(pltpu.make_async_copy(ck_hbm.at[0, pg, :, h, :], kbuf.at[slot, i], sem.at[0, slot]))
            out.append(pltpu.make_async_copy(cv_hbm.at[0, pg, :, h, :], vbuf.at[slot, i], sem.at[1, slot]))
        return out

    slot = step % 2

    @pl.when(step == 0)
    def _():
        for cp in copies(step, 0):
            cp.start()

    @pl.when(step + 1 < n_steps)
    def _():
        for cp in copies(step + 1, 1 - slot):
            cp.start()

    for cp in copies(step, slot):
        cp.wait()

    r = SAMPLE_ROWS
    scale = hd ** -0.5
    kn = kn_ref[...]
    vn = vn_ref[...]
    row = lax.broadcasted_iota(jnp.int32, (r, 1), 0)
    o_ref[...] = jnp.zeros_like(o_ref)
    for t in range(n_tok):
        q_t = q_ref[t:t + 1, :]
        s_own = jnp.sum(kn * q_t, axis=-1, keepdims=True) * scale
        s_own = jnp.where(row <= t, s_own, NEG_BIG)
        s_pg = [jnp.sum(kbuf[slot, t * n_pg + j] * q_t, axis=-1, keepdims=True) * scale for j in range(n_pg)]
        m = jnp.max(s_own, axis=0, keepdims=True)
        for s in s_pg:
            m = jnp.maximum(m, jnp.max(s, axis=0, keepdims=True))
        p_own = jnp.exp(s_own - m)
        l = jnp.sum(p_own, axis=0, keepdims=True)
        acc = jnp.sum(p_own * vn, axis=0, keepdims=True)
        for j in range(n_pg):
            p = jnp.exp(s_pg[j] - m)
            l = l + jnp.sum(p, axis=0, keepdims=True)
            acc = acc + jnp.sum(p * vbuf[slot, t * n_pg + j], axis=0, keepdims=True)
        o_ref[t:t + 1, :] = acc / l


def sample_attention(q8, kn8, vn8, cache_k, cache_v, pp_flat, *, batch, heads, hd, n_tok, page):
    ppb = MOBA_BLOCK // page
    n_pg = MOBA_TOPK * ppb
    n_steps = batch * heads
    small = pl.BlockSpec((SAMPLE_ROWS, hd), lambda s, pp: (s // heads, s % heads))
    return pl.pallas_call(
        functools.partial(_sample_attn_kernel, n_steps=n_steps, heads=heads, n_tok=n_tok, n_pg=n_pg, hd=hd),
        grid_spec=pltpu.PrefetchScalarGridSpec(
            num_scalar_prefetch=1,
            grid=(n_steps,),
            in_specs=[small, small, small, pl.BlockSpec(memory_space=pl.ANY), pl.BlockSpec(memory_space=pl.ANY)],
            out_specs=small,
            scratch_shapes=[pltpu.VMEM((2, n_tok * n_pg, page, hd), F32),
                            pltpu.VMEM((2, n_tok * n_pg, page, hd), F32),
                            pltpu.SemaphoreType.DMA((2, 2))],
        ),
        out_shape=jax.ShapeDtypeStruct((batch * SAMPLE_ROWS, heads * hd), F32),
        compiler_params=_cparams("arbitrary"),
        name="sample_attn",
    )(pp_flat, q8, kn8, vn8, cache_k, cache_v)


def _norm_router_kernel(xp_ref, xr_ref, w_ref, wr_ref, h_ref, info_ref, *, n_experts, n_head_tiles):
    x = jnp.where(pl.program_id(0) < n_head_tiles, xp_ref[...], xr_ref[...])
    y = x * lax.rsqrt(jnp.mean(x * x, axis=-1, keepdims=True) + RMS_EPS) * w_ref[...]
    h_ref[...] = y
    logits = jnp.dot(y, wr_ref[...], precision=lax.Precision.HIGHEST, preferred_element_type=F32)
    r, n = logits.shape
    col = lax.broadcasted_iota(jnp.int32, (r, n), 1)
    lg = jnp.where(col < n_experts, logits, -jnp.inf)
    m1 = jnp.max(lg, axis=-1, keepdims=True)
    i1 = jnp.min(jnp.where(lg == m1, col, n), axis=-1, keepdims=True)
    lg2 = jnp.where(col == i1, -jnp.inf, lg)
    m2 = jnp.max(lg2, axis=-1, keepdims=True)
    i2 = jnp.min(jnp.where(lg2 == m2, col, n), axis=-1, keepdims=True)
    e2 = jnp.exp(m2 - m1)
    den = 1.0 + e2
    w1 = 1.0 / den
    w2 = e2 / den
    info = jnp.where(col == 0, i1.astype(F32),
                     jnp.where(col == 1, i2.astype(F32),
                               jnp.where(col == 2, w1, jnp.where(col == 3, w2, 0.0))))
    info_ref[...] = info


def _head_rest_specs(tm, d, n_head_tiles):
    def head(i, *_):
        return (jnp.minimum(i, n_head_tiles - 1), 0)

    def rest(i, *_):
        return (jnp.maximum(i - n_head_tiles, 0), 0)

    return pl.BlockSpec((tm, d), head), pl.BlockSpec((tm, d), rest)


def norm_router(xp, xr, w, w_router):
    d = xp.shape[1]
    m = xp.shape[0] + xr.shape[0]
    e = w_router.shape[1]
    tm = ROW_ALIGN
    assert xp.shape[0] % tm == 0 and xr.shape[0] % tm == 0
    npt = xp.shape[0] // tm
    wr = jnp.zeros((d, 128), F32).at[:, :e].set(w_router)
    head, rest = _head_rest_specs(tm, d, npt)
    return pl.pallas_call(
        functools.partial(_norm_router_kernel, n_experts=e, n_head_tiles=npt),
        grid=(m // tm,),
        in_specs=[head, rest, pl.BlockSpec((1, d), lambda i: (0, 0)), pl.BlockSpec((d, 128), lambda i: (0, 0))],
        out_specs=[pl.BlockSpec((tm, d), lambda i: (i, 0)), pl.BlockSpec((tm, 128), lambda i: (i, 0))],
        out_shape=[jax.ShapeDtypeStruct((m, d), F32), jax.ShapeDtypeStruct((m, 128), F32)],
        compiler_params=_cparams("arbitrary"),
        name="norm_router",
    )(xp, xr, w.reshape(1, d), wr)


GATHER_TILE = 256


def _start_rows(src_hbm, dst_ref, sem, row_of):
    def start(r, c):
        pltpu.make_async_copy(src_hbm.at[pl.ds(row_of(r), 1), :], dst_ref.at[pl.ds(r, 1), :], sem).start()
        return c

    lax.fori_loop(0, dst_ref.shape[0], start, 0, unroll=8)


def _wait_rows(dst_ref, sem):
    pltpu.make_async_copy(dst_ref, dst_ref, sem).wait()


def _gather_cast_kernel(src_ref, ntile_ref, h_hbm, o_ref, buf_ref, sem):
    i = pl.program_id(0)
    n = ntile_ref[0]

    def issue(tile, slot):
        base = tile * GATHER_TILE
        _start_rows(h_hbm, buf_ref.at[slot], sem.at[slot], lambda r: src_ref[base + r])

    @pl.when(jnp.logical_and(i == 0, n > 0))
    def _():
        issue(0, 0)

    @pl.when(i + 1 < n)
    def _():
        issue(i + 1, (i + 1) % 2)

    @pl.when(i < n)
    def _():
        _wait_rows(buf_ref.at[i % 2], sem.at[i % 2])
        o_ref[...] = buf_ref[i % 2].astype(o_ref.dtype)

    @pl.when(i >= n)
    def _():
        o_ref[...] = jnp.zeros_like(o_ref)


def gather_cast(h, src_rows, n_tiles_used):
    s_rows = src_rows.shape[0]
    d = h.shape[1]
    return pl.pallas_call(
        _gather_cast_kernel,
        grid_spec=pltpu.PrefetchScalarGridSpec(
            num_scalar_prefetch=2,
            grid=(s_rows // GATHER_TILE,),
            in_specs=[pl.BlockSpec(memory_space=pl.ANY)],
            out_specs=pl.BlockSpec((GATHER_TILE, d), lambda i, s, n: (i, 0)),
            scratch_shapes=[pltpu.VMEM((2, GATHER_TILE, d), F32), pltpu.SemaphoreType.DMA((2,))],
        ),
        out_shape=jax.ShapeDtypeStruct((s_rows, d), BF16),
        compiler_params=_cparams("arbitrary"),
        name="gather_cast",
    )(src_rows, n_tiles_used, h)


EXPERT_SUB = 128
EXPERT_MAIN = 8
SEGMENT_ROWS = 2304


def _experts_kernel(seg_e_ref, seg_rows_ref, nseg_ref, x_ref, wg_ref, wu_ref, wd_ref, o_ref,
                    wgb_ref, wub_ref, wdb_ref):
    s = pl.program_id(0)
    f = pl.program_id(1)

    @pl.when(f == 0)
    def _():
        o_ref[...] = jnp.zeros_like(o_ref)

    def tiles(starts, size):
        gu = []
        for r0 in starts:
            xs = x_ref[pl.ds(r0, size), :]
            gu.append((_dot(xs, wgb_ref[...]), _dot(xs, wub_ref[...])))
        for r0, (g, u) in zip(starts, gu):
            a = (_silu(g) * u).astype(BF16)
            o_ref[pl.ds(r0, size), :] += _dot(a, wdb_ref[...])

    @pl.when(s < nseg_ref[0])
    def _():
        wgb_ref[...] = wg_ref[...].astype(BF16)
        wub_ref[...] = wu_ref[...].astype(BF16)
        wdb_ref[...] = wd_ref[...].astype(BF16)
        sub = EXPERT_SUB
        n_sub = (seg_rows_ref[s] + sub - 1) // sub
        n_main = n_sub // EXPERT_MAIN

        def body(i, c):
            r0 = pl.multiple_of(i * (EXPERT_MAIN * sub), EXPERT_MAIN * sub)
            half = EXPERT_MAIN // 2 * sub
            tiles([r0, r0 + half], half)
            return c

        lax.fori_loop(0, n_main, body, 0)
        done = n_main * EXPERT_MAIN
        size = EXPERT_MAIN // 2
        while size >= 1:
            take = ((n_sub - done) // size) > 0

            @pl.when(take)
            def _(done=done, size=size):
                tiles([pl.multiple_of(done * sub, sub)], size * sub)

            done = done + jnp.where(take, size, 0)
            size //= 2


def experts(x_sorted, wg, wu, wd, layer, seg_e, seg_rows, nseg, *, seg_cap):
    s_rows, d = x_sorted.shape
    s_max = s_rows // seg_cap
    f_dim = wg.shape[3]
    tf = _pick(f_dim, (256, 128))
    nf = f_dim // tf

    def seg_blk(s, n):
        return jnp.minimum(s, n[0] - 1)

    def fblk(s, f, n):
        return jnp.where(s < n[0], f, nf - 1)

    one = pl.Buffered(1)
    in_specs = [
        pl.BlockSpec((seg_cap, d), lambda s, f, e, r, n: (seg_blk(s, n), 0), pipeline_mode=one),
        pl.BlockSpec((None, None, d, tf), lambda s, f, e, r, n: (layer, e[s], 0, fblk(s, f, n))),
        pl.BlockSpec((None, None, d, tf), lambda s, f, e, r, n: (layer, e[s], 0, fblk(s, f, n))),
        pl.BlockSpec((None, None, tf, d), lambda s, f, e, r, n: (layer, e[s], fblk(s, f, n), 0)),
    ]
    return pl.pallas_call(
        _experts_kernel,
        grid_spec=pltpu.PrefetchScalarGridSpec(
            num_scalar_prefetch=3,
            grid=(s_max, nf),
            in_specs=in_specs,
            out_specs=pl.BlockSpec((seg_cap, d), lambda s, f, e, r, n: (s, 0), pipeline_mode=one),
            scratch_shapes=[pltpu.VMEM((d, tf), BF16), pltpu.VMEM((d, tf), BF16),
                            pltpu.VMEM((tf, d), BF16)],
        ),
        out_shape=jax.ShapeDtypeStruct((s_rows, d), F32),
        compiler_params=_cparams("arbitrary", "arbitrary"),
        name="experts",
    )(seg_e, seg_rows, nseg, x_sorted, wg, wu, wd)


def _combine_norm_kernel(s1_ref, s2_ref, ys_hbm, xp_ref, xr_ref, info_ref, w_ref, op_ref, or_ref, a_ref, b_ref,
                         sem, *, n_steps, n_head_tiles):
    i = pl.program_id(0)
    tm = xp_ref.shape[0]

    def issue(tile, slot):
        base = tile * tm
        _start_rows(ys_hbm, a_ref.at[slot], sem.at[0, slot], lambda r: s1_ref[base + r])
        _start_rows(ys_hbm, b_ref.at[slot], sem.at[1, slot], lambda r: s2_ref[base + r])

    @pl.when(i == 0)
    def _():
        issue(0, 0)

    @pl.when(i + 1 < n_steps)
    def _():
        issue(i + 1, (i + 1) % 2)

    slot = i % 2
    _wait_rows(a_ref.at[slot], sem.at[0, slot])
    _wait_rows(b_ref.at[slot], sem.at[1, slot])
    info = info_ref[...]
    y = info[:, 2:3] * a_ref[slot] + info[:, 3:4] * b_ref[slot]
    x = jnp.where(i < n_head_tiles, xp_ref[...], xr_ref[...]) + y
    xn = x * lax.rsqrt(jnp.mean(x * x, axis=-1, keepdims=True) + RMS_EPS)
    out = xn * w_ref[...]

    @pl.when(i < n_head_tiles)
    def _():
        op_ref[...] = out

    @pl.when(i >= n_head_tiles)
    def _():
        or_ref[...] = out


def combine_norm(xp, xr, y_sorted, slot1, slot2, info, w):
    d = xp.shape[1]
    m = xp.shape[0] + xr.shape[0]
    tm = ROW_ALIGN
    assert xp.shape[0] % tm == 0 and xr.shape[0] % tm == 0
    npt = xp.shape[0] // tm
    head, rest = _head_rest_specs(tm, d, npt)
    return pl.pallas_call(
        functools.partial(_combine_norm_kernel, n_steps=m // tm, n_head_tiles=npt),
        grid_spec=pltpu.PrefetchScalarGridSpec(
            num_scalar_prefetch=2,
            grid=(m // tm,),
            in_specs=[pl.BlockSpec(memory_space=pl.ANY), head, rest,
                      pl.BlockSpec((tm, 128), lambda i, a, b: (i, 0)),
                      pl.BlockSpec((1, d), lambda i, a, b: (0, 0))],
            out_specs=[head, rest],
            scratch_shapes=[pltpu.VMEM((2, tm, d), F32), pltpu.VMEM((2, tm, d), F32),
                            pltpu.SemaphoreType.DMA((2, 2))],
        ),
        out_shape=[jax.ShapeDtypeStruct(xp.shape, F32), jax.ShapeDtypeStruct(xr.shape, F32)],
        compiler_params=_cparams("arbitrary"),
        name="combine_norm",
    )(slot1, slot2, y_sorted, xp, xr, info, w.reshape(1, d))


def _route_tables(info, n_real, n_experts, seg_cap):
    m = info.shape[0]
    e = info[:, :MOE_TOPK].astype(jnp.int32)
    valid = (jnp.arange(m) < n_real)[:, None]
    onehot = ((e[:, :, None] == jnp.arange(n_experts)[None, None, :]) & valid[:, :, None])
    onehot = onehot.reshape(m * MOE_TOPK, n_experts).astype(jnp.int32)
    rank = jnp.sum((jnp.cumsum(onehot, axis=0) - onehot) * onehot, axis=1)
    counts = jnp.sum(onehot, axis=0)
    nseg_e = (counts + seg_cap - 1) // seg_cap
    seg_start = jnp.cumsum(nseg_e) - nseg_e
    nseg = jnp.sum(nseg_e)
    s_max = (m * MOE_TOPK) // seg_cap + n_experts
    flat_e = e.reshape(-1)
    flat_valid = jnp.broadcast_to(valid, (m, MOE_TOPK)).reshape(-1)
    slot = (seg_start[flat_e] + rank // seg_cap) * seg_cap + rank % seg_cap
    slot = jnp.where(flat_valid, slot, 0)
    sid = jnp.arange(s_max)
    owner = jnp.argmax((sid[:, None] >= seg_start[None, :]) & (sid[:, None] < (seg_start + nseg_e)[None, :]), axis=1)
    seg_e = jnp.where(sid < nseg, owner, owner[jnp.maximum(nseg - 1, 0)]).astype(jnp.int32)
    within = sid - seg_start[seg_e]
    seg_rows = jnp.clip(counts[seg_e] - within * seg_cap, 0, seg_cap)
    seg_rows = jnp.where(sid < nseg, seg_rows, 0).astype(jnp.int32)
    tok = jnp.repeat(jnp.arange(m, dtype=jnp.int32), MOE_TOPK)
    src = jnp.full((s_max * seg_cap,), m - 1, jnp.int32)
    src = src.at[jnp.where(flat_valid, slot, s_max * seg_cap)].set(tok, mode="drop")
    n_tiles_used = (nseg * (seg_cap // GATHER_TILE)).astype(jnp.int32).reshape(1)
    slots = slot.reshape(m, MOE_TOPK).astype(jnp.int32)
    return src, n_tiles_used, seg_e, seg_rows, nseg.astype(jnp.int32).reshape(1), slots[:, 0], slots[:, 1]


def kernel(x_prompt, x_sample, state_ret, cache_k, cache_v, page_table, norm_w, final_norm_w,
           ret_w_in, ret_w_out, moba_w_qkv, moba_w_out, ffn_w_gate, ffn_w_up, ffn_w_down,
           moe_w_router, moe_w_gate, moe_w_up, moe_w_down):
    bp, sp, d = x_prompt.shape
    bs, ts, _ = x_sample.shape
    _, _, r_heads, dk, dv = state_ret.shape
    _, n_pool, page, m_heads, hd = cache_k.shape
    n_pages = page_table.shape[1]
    past_len = n_pages * page
    n_experts = moe_w_router.shape[2]
    assert sp % RET_CHUNK == 0 and ts < RET_CHUNK and ts <= SAMPLE_ROWS
    assert MOBA_BLOCK % page == 0 and past_len % MOBA_BLOCK == 0 and sp % MOBA_BLOCK == 0

    n_p = bp * sp
    n_s = bs * ts
    n_real = n_p + n_s
    xp = x_prompt.reshape(n_p, d)
    xs = x_sample.reshape(n_s, d)

    hp = rmsnorm(xp, norm_w[0, 0], BF16)
    hs = rmsnorm(xs, norm_w[0, 0], F32)
    n_qk = 2 * r_heads * dk
    qk = matmul(hp, ret_w_in, 0, n_off=0, n=n_qk, name="ret_qk_proj")
    vg = matmul(hp, ret_w_in, 0, n_off=n_qk, n=2 * r_heads * dv, out_dtype=BF16, name="ret_vg_proj")
    pos_p = jnp.arange(sp, dtype=jnp.int32)
    zero_state = jnp.zeros((bp, r_heads, dk, dv), F32)
    y_p, st_p = retention(qk, vg, zero_state, pos_p, batch=bp, n_chunks=sp // RET_CHUNK,
                          chunk=RET_CHUNK, valid=RET_CHUNK, heads=r_heads, dk=dk, dv=dv)
    qkvg_s = matmul_precise(hs, ret_w_in, 0, name="ret_in_proj_s")

    def pad_chunk(a2):
        a3 = jnp.pad(a2.reshape(bs, ts, -1), ((0, 0), (0, RET_CHUNK - ts), (0, 0)))
        return a3.reshape(bs * RET_CHUNK, -1)

    pos_s = past_len + jnp.arange(RET_CHUNK, dtype=jnp.int32)
    y_s, st_s = retention(pad_chunk(qkvg_s[:, :n_qk]), pad_chunk(qkvg_s[:, n_qk:]), state_ret[0], pos_s,
                          batch=bs, n_chunks=1, chunk=RET_CHUNK, valid=ts, heads=r_heads, dk=dk, dv=dv,
                          precise=True)
    y_s = y_s.reshape(bs, RET_CHUNK, -1)[:, :ts].reshape(n_s, -1)
    xp = matmul(y_p, ret_w_out, 0, res=xp, name="ret_out_proj")
    xs = matmul_precise(y_s, ret_w_out, 0, res=xs, name="ret_out_proj_s")

    hp = rmsnorm(xp, norm_w[0, 1], BF16)
    hs = rmsnorm(xs, norm_w[0, 1], F32)
    xp = matmul(gate_up(hp, ffn_w_gate, ffn_w_up, 0), ffn_w_down, 0, res=xp, name="ffn_down")
    xs = matmul_precise(gate_up_precise(hs, ffn_w_gate, ffn_w_up, 0), ffn_w_down, 0, res=xs, name="ffn_down_s")

    hp = rmsnorm(xp, norm_w[1, 0], BF16)
    hs = rmsnorm(xs, norm_w[1, 0], F32)
    q = matmul(hp, moba_w_qkv, 0, n_off=0, n=d, name="moba_q_proj")
    k = matmul(hp, moba_w_qkv, 0, n_off=d, n=d, name="moba_k_proj")
    v = matmul(hp, moba_w_qkv, 0, n_off=2 * d, n=d, name="moba_v_proj")
    o_p = moba_prompt(q, k, v, batch=bp, seq=sp, heads=m_heads, hd=hd)
    qkv_s = matmul_precise(hs, moba_w_qkv, 0, name="moba_qkv_proj_s")
    q_s, k_s, v_s = qkv_s[:, :d], qkv_s[:, d:2 * d], qkv_s[:, 2 * d:]

    def pad8(a2):
        return jnp.pad(a2.reshape(bs, ts, d), ((0, 0), (0, SAMPLE_ROWS - ts), (0, 0))).reshape(bs * SAMPLE_ROWS, d)

    means = page_block_means(cache_k, page_table.reshape(-1), batch=bs, n_pages=n_pages)
    sel = sample_gate(q_s.reshape(bs, ts, m_heads, hd), means, n_tok=ts)
    sel = sel.reshape(bs, ts, MOBA_TOPK, m_heads).transpose(0, 3, 1, 2)
    ppb = MOBA_BLOCK // page
    lpage = sel[..., None] * ppb + jnp.arange(ppb, dtype=jnp.int32)
    pp = page_table[jnp.arange(bs)[:, None, None, None], lpage.reshape(bs, m_heads, ts, MOBA_TOPK * ppb)]
    o_s8 = sample_attention(pad8(q_s), pad8(k_s), pad8(v_s), cache_k, cache_v, pp.reshape(-1).astype(jnp.int32),
                            batch=bs, heads=m_heads, hd=hd, n_tok=ts, page=page)
    o_s = o_s8.reshape(bs, SAMPLE_ROWS, d)[:, :ts].reshape(n_s, d)
    xp = matmul(o_p, moba_w_out, 0, res=xp, name="moba_out_proj")
    xs = matmul_precise(o_s, moba_w_out, 0, res=xs, name="moba_out_proj_s")

    m = -(-(n_real + 1) // ROW_ALIGN) * ROW_ALIGN
    xr = jnp.concatenate([xs, jnp.zeros((m - n_real, d), F32)], axis=0)
    hf, info = norm_router(xp, xr, norm_w[1, 1], moe_w_router[0])
    seg_cap = SEGMENT_ROWS if m * MOE_TOPK >= 4 * SEGMENT_ROWS else 2 * GATHER_TILE
    src, n_tiles_used, seg_e, seg_rows, nseg, slot1, slot2 = _route_tables(info, n_real, n_experts, seg_cap)
    x_sorted = gather_cast(hf, src, n_tiles_used)
    y_sorted = experts(x_sorted, moe_w_gate, moe_w_up, moe_w_down, 0, seg_e, seg_rows, nseg, seg_cap=seg_cap)
    y_p, y_r = combine_norm(xp, xr, y_sorted, slot1, slot2, info, final_norm_w)

    y_prompt = y_p.reshape(bp, sp, d)
    y_sample = y_r[:n_s].reshape(bs, ts, d)
    kv_shape_p = (1, bp, sp, m_heads, hd)
    kv_shape_s = (1, bs, ts, m_heads, hd)
    return (y_prompt, y_sample, st_p[None], st_s[None],
            k.reshape(kv_shape_p), v.reshape(kv_shape_p), k_s.reshape(kv_shape_s), v_s.reshape(kv_shape_s))
```

```python
import functools

import jax
import jax.numpy as jnp
from jax import lax
from jax.experimental import pallas as pl
from jax.experimental.pallas import tpu as pltpu

RET_CHUNK = 128
ROPE_BASE = 10000.0
MOBA_BLOCK = 256
MOBA_TOPK = 3
MOE_TOPK = 2
RMS_EPS = 1e-6
GN_EPS = 1e-5

V7X_VMEM_LIMIT_BYTES = 56 * 1024 * 1024
ROW_ALIGN = 256
NEG_BIG = -1e30

F32 = jnp.float32
BF16 = jnp.bfloat16


def _cparams(*sem):
    return pltpu.CompilerParams(dimension_semantics=sem, vmem_limit_bytes=V7X_VMEM_LIMIT_BYTES)


def _pick(n, prefs):
    for p in prefs:
        if p <= n and n % p == 0:
            return p
    return n


def _silu(g):
    return g * (1.0 / (1.0 + jnp.exp(-g)))


def _dot(a, b):
    return jnp.dot(a, b, preferred_element_type=F32)


def _dot_nt(a, b, precision=None):
    return lax.dot_general(a, b, (((1,), (1,)), ((), ())), precision=precision,
                           preferred_element_type=F32)


def _split(x):
    hi = x.astype(BF16)
    return hi, (x - hi.astype(F32)).astype(BF16)


def _dot3(a, b):
    ah, al = _split(a)
    bh, bl = _split(b)
    return _dot(ah, bh) + (_dot(al, bh) + _dot(ah, bl))


def _dot3_nt(a, b):
    ah, al = _split(a)
    bh, bl = _split(b)
    return _dot_nt(ah, bh) + (_dot_nt(al, bh) + _dot_nt(ah, bl))


def _dot1(a, b):
    return _dot(a.astype(BF16), b.astype(BF16))


def _dot1_nt(a, b):
    return _dot_nt(a.astype(BF16), b.astype(BF16))


def _rmsnorm_kernel(x_ref, w_ref, o_ref):
    x = x_ref[...]
    y = x * lax.rsqrt(jnp.mean(x * x, axis=-1, keepdims=True) + RMS_EPS)
    o_ref[...] = (y * w_ref[...]).astype(o_ref.dtype)


def rmsnorm(x, w, out_dtype):
    m, d = x.shape
    tm = _pick(m, (256, 128, 64, 32, 16, 8))
    return pl.pallas_call(
        _rmsnorm_kernel,
        grid=(m // tm,),
        in_specs=[pl.BlockSpec((tm, d), lambda i: (i, 0)), pl.BlockSpec((1, d), lambda i: (0, 0))],
        out_specs=pl.BlockSpec((tm, d), lambda i: (i, 0)),
        out_shape=jax.ShapeDtypeStruct((m, d), out_dtype),
        compiler_params=_cparams("parallel"),
        name="rmsnorm",
    )(x, w.reshape(1, d))


def _mm_kernel(*refs, has_res):
    if has_res:
        x_ref, w_ref, r_ref, o_ref, wb_ref = refs
    else:
        x_ref, w_ref, o_ref, wb_ref = refs

    @pl.when(pl.program_id(1) == 0)
    def _():
        wb_ref[...] = w_ref[...].astype(BF16)

    acc = _dot(x_ref[...], wb_ref[...])
    if has_res:
        acc = acc + r_ref[...]
    o_ref[...] = acc.astype(o_ref.dtype)


TILE_VMEM_BUDGET = 44 * 1024 * 1024


def _matmul_tiles(m, k, n, n_w, out_bytes, has_res):
    best = None
    for tm in (2048, 1056, 1024, 528, 512, 256, 128, 64, 32, 16):
        if m % tm:
            continue
        for tn in (1024, 512, 256, 128):
            if n % tn:
                continue
            need = (2 * tm * k * 2 + n_w * (2 * k * tn * 4 + k * tn * 2)
                    + 2 * tm * tn * out_bytes + (2 * tm * tn * 4 if has_res else 0))
            score = (-(m // tm) * (n // tn), tn)
            if need <= TILE_VMEM_BUDGET and (best is None or score > best[0]):
                best = (score, (tm, tn))
    assert best is not None
    return best[1]


def matmul(x, w3, layer, *, name, n_off=0, n=None, res=None, out_dtype=F32):
    m, k = x.shape
    n = w3.shape[2] if n is None else n
    tm, tn = _matmul_tiles(m, k, n, 1, jnp.dtype(out_dtype).itemsize, res is not None)
    assert n_off % tn == 0
    off = n_off // tn
    in_specs = [pl.BlockSpec((tm, k), lambda j, i: (i, 0)),
                pl.BlockSpec((None, k, tn), lambda j, i: (layer, 0, j + off))]
    args = [x, w3]
    if res is not None:
        in_specs.append(pl.BlockSpec((tm, tn), lambda j, i: (i, j)))
        args.append(res)
    return pl.pallas_call(
        functools.partial(_mm_kernel, has_res=res is not None),
        grid=(n // tn, m // tm),
        in_specs=in_specs,
        out_specs=pl.BlockSpec((tm, tn), lambda j, i: (i, j)),
        out_shape=jax.ShapeDtypeStruct((m, n), out_dtype),
        scratch_shapes=[pltpu.VMEM((k, tn), BF16)],
        compiler_params=_cparams("arbitrary", "arbitrary"),
        name=name,
    )(*args)


def _gateup_kernel(x_ref, wg_ref, wu_ref, o_ref, wgb_ref, wub_ref):
    @pl.when(pl.program_id(1) == 0)
    def _():
        wgb_ref[...] = wg_ref[...].astype(BF16)
        wub_ref[...] = wu_ref[...].astype(BF16)

    x = x_ref[...]
    g = _dot(x, wgb_ref[...])
    u = _dot(x, wub_ref[...])
    o_ref[...] = (_silu(g) * u).astype(o_ref.dtype)


def gate_up(x, wg3, wu3, layer):
    m, k = x.shape
    f = wg3.shape[2]
    tm, tn = _matmul_tiles(m, k, f, 2, 2, False)
    wspec = pl.BlockSpec((None, k, tn), lambda j, i: (layer, 0, j))
    return pl.pallas_call(
        _gateup_kernel,
        grid=(f // tn, m // tm),
        in_specs=[pl.BlockSpec((tm, k), lambda j, i: (i, 0)), wspec, wspec],
        out_specs=pl.BlockSpec((tm, tn), lambda j, i: (i, j)),
        out_shape=jax.ShapeDtypeStruct((m, f), BF16),
        scratch_shapes=[pltpu.VMEM((k, tn), BF16), pltpu.VMEM((k, tn), BF16)],
        compiler_params=_cparams("arbitrary", "arbitrary"),
        name="gate_up",
    )(x, wg3, wu3)


PRECISE_W_TILE_BYTES = 8 * 1024 * 1024


def _precise_tn(k, n, n_w=1):
    fits = [tn for tn in range(128, n + 1, 128) if n % tn == 0 and n_w * k * tn * 4 <= PRECISE_W_TILE_BYTES]
    assert fits
    return fits[-1]


def _mm3_kernel(*refs, has_res):
    if has_res:
        x_ref, w_ref, r_ref, o_ref = refs
    else:
        x_ref, w_ref, o_ref = refs
    acc = _dot3(x_ref[...], w_ref[...])
    if has_res:
        acc = acc + r_ref[...]
    o_ref[...] = acc


def matmul_precise(x, w3, layer, *, name, n_off=0, n=None, res=None):
    m, k = x.shape
    n = w3.shape[2] if n is None else n
    tn = _precise_tn(k, n)
    assert n_off % tn == 0
    off = n_off // tn
    in_specs = [pl.BlockSpec((m, k), lambda j: (0, 0)),
                pl.BlockSpec((None, k, tn), lambda j: (layer, 0, j + off))]
    args = [x, w3]
    if res is not None:
        in_specs.append(pl.BlockSpec((m, tn), lambda j: (0, j)))
        args.append(res)
    return pl.pallas_call(
        functools.partial(_mm3_kernel, has_res=res is not None),
        grid=(n // tn,),
        in_specs=in_specs,
        out_specs=pl.BlockSpec((m, tn), lambda j: (0, j)),
        out_shape=jax.ShapeDtypeStruct((m, n), F32),
        compiler_params=_cparams("arbitrary"),
        name=name,
    )(*args)


def _gateup3_kernel(x_ref, wg_ref, wu_ref, o_ref):
    x = x_ref[...]
    g = _dot3(x, wg_ref[...])
    u = _dot3(x, wu_ref[...])
    o_ref[...] = _silu(g) * u


def gate_up_precise(x, wg3, wu3, layer):
    m, k = x.shape
    f = wg3.shape[2]
    tn = _precise_tn(k, f, 2)
    wspec = pl.BlockSpec((None, k, tn), lambda j: (layer, 0, j))
    return pl.pallas_call(
        _gateup3_kernel,
        grid=(f // tn,),
        in_specs=[pl.BlockSpec((m, k), lambda j: (0, 0)), wspec, wspec],
        out_specs=pl.BlockSpec((m, tn), lambda j: (0, j)),
        out_shape=jax.ShapeDtypeStruct((m, f), F32),
        compiler_params=_cparams("arbitrary"),
        name="gate_up_precise",
    )(x, wg3, wu3)


def _retention_kernel(q_ref, k_ref, v_ref, g_ref, cos_ref, sin_ref, dec_ref, qd_ref, kd_ref,
                      cd_ref, s0_ref, y_ref, sout_ref, state_ref, *, n_chunks, dk, dv, hg, precise):
    c = pl.program_id(2)

    @pl.when(c == 0)
    def _():
        state_ref[...] = s0_ref[0]

    half = dk // 2
    cos = cos_ref[...]
    sin = sin_ref[...]

    def rot(x):
        x1, x2 = x[:, :half], x[:, half:]
        return jnp.concatenate([x1 * cos - x2 * sin, x1 * sin + x2 * cos], axis=-1)

    qs, ks, vs, sts = [], [], [], []
    for i in range(hg):
        qs.append(rot(q_ref[:, i * dk:(i + 1) * dk]))
        ks.append(rot(k_ref[:, i * dk:(i + 1) * dk]) * (dk ** -0.5))
        vs.append(v_ref[:, i * dv:(i + 1) * dv])
        sts.append(state_ref[i])
    mm, mm_nt = (_dot3, _dot3_nt) if precise else (_dot1, _dot1_nt)
    raw = [mm_nt(qs[i], ks[i]) for i in range(hg)]
    crosses = [mm(qs[i], sts[i]) for i in range(hg)]
    for i in range(hg):
        kdt = jnp.transpose(ks[i] * kd_ref[i])
        state_ref[i] = sts[i] * cd_ref[i] + mm(kdt, vs[i])
    inners = [mm(raw[i] * dec_ref[i], vs[i]) for i in range(hg)]
    for i in range(hg):
        y = inners[i] + crosses[i] * qd_ref[i]
        mu = jnp.mean(y, axis=-1, keepdims=True)
        d = y - mu
        var = jnp.mean(d * d, axis=-1, keepdims=True)
        yn = d * lax.rsqrt(var + GN_EPS)
        g = g_ref[:, i * dv:(i + 1) * dv].astype(F32)
        y_ref[:, i * dv:(i + 1) * dv] = (_silu(g) * yn).astype(y_ref.dtype)

    @pl.when(c == n_chunks - 1)
    def _():
        sout_ref[0] = state_ref[...]


def _retention_tables(heads, chunk, valid):
    lg = jnp.log1p(-jnp.exp2(-5.0 - jnp.arange(heads, dtype=F32)))
    i = jnp.arange(chunk, dtype=F32)
    ok = i < valid
    diff = i[:, None] - i[None, :]
    dec = jnp.where((diff[None] >= 0) & ok[None, :, None] & ok[None, None, :],
                    jnp.exp(jnp.maximum(diff, 0.0)[None] * lg[:, None, None]), 0.0)
    qd = jnp.where(ok[None, :], jnp.exp((i + 1.0)[None, :] * lg[:, None]), 0.0)
    kd = jnp.where(ok[None, :], jnp.exp((valid - 1.0 - i)[None, :] * lg[:, None]), 0.0)
    cd = jnp.exp(valid * lg)
    return dec, qd[:, :, None], kd[:, :, None], cd[:, None, None]


def _rope_tables(pos, half):
    inv = ROPE_BASE ** (-jnp.arange(half, dtype=F32) / half)
    ang = pos.astype(F32)[:, None] * inv[None, :]
    return jnp.cos(ang), jnp.sin(ang)


def retention(qk, vg, state0, pos, *, batch, n_chunks, chunk, valid, heads, dk, dv, precise=False):
    rows = batch * n_chunks * chunk
    assert qk.shape[0] >= rows and vg.shape[0] >= rows
    hg = 4 if heads % 4 == 0 else (2 if heads % 2 == 0 else 1)
    ng = heads // hg
    dec, qd, kd, cd = _retention_tables(heads, chunk, valid)
    cos, sin = _rope_tables(pos, dk // 2)
    row = lambda b, h, c: b * n_chunks + c
    in_specs = [
        pl.BlockSpec((chunk, hg * dk), lambda b, h, c: (row(b, h, c), h)),
        pl.BlockSpec((chunk, hg * dk), lambda b, h, c: (row(b, h, c), ng + h)),
        pl.BlockSpec((chunk, hg * dv), lambda b, h, c: (row(b, h, c), h)),
        pl.BlockSpec((chunk, hg * dv), lambda b, h, c: (row(b, h, c), ng + h)),
        pl.BlockSpec((chunk, dk // 2), lambda b, h, c: (c, 0)),
        pl.BlockSpec((chunk, dk // 2), lambda b, h, c: (c, 0)),
        pl.BlockSpec((hg, chunk, chunk), lambda b, h, c: (h, 0, 0)),
        pl.BlockSpec((hg, chunk, 1), lambda b, h, c: (h, 0, 0)),
        pl.BlockSpec((hg, chunk, 1), lambda b, h, c: (h, 0, 0)),
        pl.BlockSpec((hg, 1, 1), lambda b, h, c: (h, 0, 0)),
        pl.BlockSpec((1, hg, dk, dv), lambda b, h, c: (b, h, 0, 0)),
    ]
    out_specs = [
        pl.BlockSpec((chunk, hg * dv), lambda b, h, c: (row(b, h, c), h)),
        pl.BlockSpec((1, hg, dk, dv), lambda b, h, c: (b, h, 0, 0)),
    ]
    return pl.pallas_call(
        functools.partial(_retention_kernel, n_chunks=n_chunks, dk=dk, dv=dv, hg=hg, precise=precise),
        grid=(batch, ng, n_chunks),
        in_specs=in_specs,
        out_specs=out_specs,
        out_shape=[jax.ShapeDtypeStruct((rows, heads * dv), F32 if precise else BF16),
                   jax.ShapeDtypeStruct((batch, heads, dk, dv), F32)],
        scratch_shapes=[pltpu.VMEM((hg, dk, dv), F32)],
        compiler_params=_cparams("arbitrary", "arbitrary", "arbitrary"),
        name="retention",
    )(qk, qk, vg, vg, cos, sin, dec, qd, kd, cd, state0)


def _top_rows(gate_t, n_valid, n_sel):
    n, r = gate_t.shape
    row = lax.broadcasted_iota(jnp.int32, (n, r), 0)
    valid = row < n_valid
    g = jnp.where(valid, gate_t, -jnp.inf)
    rank = jnp.zeros((n, r), F32)
    for j in range(n):
        gj = g[j:j + 1, :]
        tie = jnp.where(gj == g, jnp.where(row > j, 1.0, 0.0), 0.0)
        rank = rank + jnp.where(gj > g, 1.0, tie)
    return jnp.where(valid, jnp.where(rank < n_sel, 1.0, 0.0), 0.0)


LOG2E = 1.4426950408889634


def _moba_prompt_kernel(q_ref, k_ref, v_ref, o_ref, kx_ref, vt_ref, means_ref, qx_ref, m_ref, l_ref,
                        acc_ref, sc_ref, *, n_blocks, hd, hg):
    t = pl.program_id(2)
    blk = MOBA_BLOCK
    kw = 2 * hd
    assert n_blocks <= hd
    c_exp = (hd ** -0.5) * LOG2E

    @pl.when(t == 0)
    def _():
        lane = lax.broadcasted_iota(jnp.int32, (blk, hd), 1)
        for n in range(n_blocks):
            rows = slice(n * blk, (n + 1) * blk)
            means_ref[n:n + 1, :] = jnp.mean(k_ref[rows, :], axis=0, keepdims=True)
            block_id = jnp.where(lane == n, 1.0, 0.0).astype(BF16)
            for i in range(hg):
                kx_ref[rows, i * kw:i * kw + hd] = k_ref[rows, i * hd:(i + 1) * hd].astype(BF16)
                kx_ref[rows, i * kw + hd:(i + 1) * kw] = block_id
                vt_ref[i * hd:(i + 1) * hd, rows] = jnp.transpose(v_ref[rows, i * hd:(i + 1) * hd]).astype(BF16)
        qx_ref[...] = jnp.zeros_like(qx_ref)

    r0 = pl.multiple_of(t * blk, blk)
    ki = lax.broadcasted_iota(jnp.int32, (blk, blk), 0)
    qi = lax.broadcasted_iota(jnp.int32, (blk, blk), 1)

    heads = [slice(i * hd, (i + 1) * hd) for i in range(hg)]
    gates = [_dot3_nt(means_ref[:, cs], q_ref[:, cs]) for cs in heads]
    for i, cs in enumerate(heads):
        qx_ref[i, 0:hd, :] = jnp.transpose(q_ref[:, cs] * c_exp).astype(BF16)
    own = [_dot(kx_ref[pl.ds(r0, blk), i * kw:i * kw + hd], qx_ref[i, 0:hd, :]) for i in range(hg)]
    for i, cs in enumerate(heads):
        sel_t = _top_rows(gates[i], t, min(MOBA_TOPK, n_blocks))
        qx_ref[i, hd:hd + n_blocks, :] = ((1.0 - sel_t) * NEG_BIG).astype(BF16)
        s = jnp.where(ki <= qi, own[i], NEG_BIG)
        m0 = jnp.max(s, axis=0, keepdims=True)
        p = jnp.exp2(s - m0)
        m_ref[i] = m0
        l_ref[i] = jnp.sum(p, axis=0, keepdims=True)
        acc_ref[i] = _dot(vt_ref[cs, pl.ds(r0, blk)], p.astype(BF16))

    for i in range(hg):
        sc_ref[i] = _dot(kx_ref[0:blk, i * kw:(i + 1) * kw], qx_ref[i])

    def body(n, c):
        rn = pl.multiple_of(n * blk, blk)
        rx = pl.multiple_of(jnp.minimum(n + 1, t - 1) * blk, blk)
        ahead = [_dot(kx_ref[pl.ds(rx, blk), i * kw:(i + 1) * kw], qx_ref[i]) for i in range(hg)]
        upd = []
        for i, cs in enumerate(heads):
            s = sc_ref[i]
            m_new = jnp.maximum(m_ref[i], jnp.max(s, axis=0, keepdims=True))
            alpha = jnp.exp2(m_ref[i] - m_new)
            p = jnp.exp2(s - m_new)
            pv = _dot(vt_ref[cs, pl.ds(rn, blk)], p.astype(BF16))
            upd.append((m_new, alpha, jnp.sum(p, axis=0, keepdims=True), pv))
        for i, (m_new, alpha, psum, pv) in enumerate(upd):
            m_ref[i] = m_new
            l_ref[i] = alpha * l_ref[i] + psum
            acc_ref[i] = alpha * acc_ref[i] + pv
            sc_ref[i] = ahead[i]
        return c

    lax.fori_loop(0, t, body, 0)
    for i in range(hg):
        o_ref[:, i * hd:(i + 1) * hd] = jnp.transpose(acc_ref[i] / l_ref[i]).astype(o_ref.dtype)


def moba_prompt(q, k, v, *, batch, seq, heads, hd):
    n_blocks = seq // MOBA_BLOCK
    assert seq % MOBA_BLOCK == 0
    hg = 4 if heads % 4 == 0 else 1
    qspec = pl.BlockSpec((MOBA_BLOCK, hg * hd), lambda b, h, t: (b * n_blocks + t, h))
    kvspec = pl.BlockSpec((seq, hg * hd), lambda b, h, t: (b, h))
    return pl.pallas_call(
        functools.partial(_moba_prompt_kernel, n_blocks=n_blocks, hd=hd, hg=hg),
        grid=(batch, heads // hg, n_blocks),
        in_specs=[qspec, kvspec, kvspec],
        out_specs=qspec,
        out_shape=jax.ShapeDtypeStruct((batch * seq, heads * hd), BF16),
        scratch_shapes=[pltpu.VMEM((seq, hg * 2 * hd), BF16), pltpu.VMEM((hg * hd, seq), BF16),
                        pltpu.VMEM((n_blocks, hg * hd), F32),
                        pltpu.VMEM((hg, 2 * hd, MOBA_BLOCK), BF16),
                        pltpu.VMEM((hg, 1, MOBA_BLOCK), F32), pltpu.VMEM((hg, 1, MOBA_BLOCK), F32),
                        pltpu.VMEM((hg, hd, MOBA_BLOCK), F32),
                        pltpu.VMEM((hg, MOBA_BLOCK, MOBA_BLOCK), F32)],
        compiler_params=_cparams("arbitrary", "arbitrary", "arbitrary"),
        name="moba_prompt",
    )(q, k, v)


SAMPLE_ROWS = 8
MEANS_PAGES = 4


def _page_means_kernel(pt_ref, *refs, pages_per_block):
    kp = refs[:MEANS_PAGES]
    o_ref = refs[MEANS_PAGES]
    g = pl.program_id(1)
    for i in range(MEANS_PAGES // pages_per_block):
        tot = jnp.sum(kp[i * pages_per_block][...], axis=0)
        for j in range(1, pages_per_block):
            tot = tot + jnp.sum(kp[i * pages_per_block + j][...], axis=0)
        o_ref[0, g * (MEANS_PAGES // pages_per_block) + i] = tot * (1.0 / MOBA_BLOCK)


def page_block_means(cache_k, pt_flat, *, batch, n_pages):
    _, _, page, heads, hd = cache_k.shape
    ppb = MOBA_BLOCK // page
    nb = n_pages // ppb
    assert MEANS_PAGES % ppb == 0 and n_pages % MEANS_PAGES == 0

    def pspec(i):
        return pl.BlockSpec((None, None, page, heads, hd),
                            lambda b, g, pt: (0, pt[b * n_pages + g * MEANS_PAGES + i], 0, 0, 0))

    return pl.pallas_call(
        functools.partial(_page_means_kernel, pages_per_block=ppb),
        grid_spec=pltpu.PrefetchScalarGridSpec(
            num_scalar_prefetch=1,
            grid=(batch, n_pages // MEANS_PAGES),
            in_specs=[pspec(i) for i in range(MEANS_PAGES)],
            out_specs=pl.BlockSpec((1, nb, heads, hd), lambda b, g, pt: (b, 0, 0, 0)),
        ),
        out_shape=jax.ShapeDtypeStruct((batch, nb, heads, hd), F32),
        compiler_params=_cparams("arbitrary", "arbitrary"),
        name="page_means",
    )(pt_flat, *([cache_k] * MEANS_PAGES))


def _sample_gate_kernel(q_ref, m_ref, idx_ref, *, n_tok, n_sel):
    means = m_ref[0]
    nb = means.shape[0]
    blk_id = lax.broadcasted_iota(jnp.int32, (nb,) + means.shape[1:2] + (1,), 0)
    for t in range(n_tok):
        g = jnp.sum(means * q_ref[0, t][None], axis=-1, keepdims=True)
        for s in range(n_sel):
            m = jnp.max(g, axis=0, keepdims=True)
            idx = jnp.min(jnp.where(g == m, blk_id, nb), axis=0, keepdims=True)
            idx_ref[0, t * n_sel + s] = idx[0]
            g = jnp.where(blk_id == idx, -jnp.inf, g)


def sample_gate(q4, means, *, n_tok):
    batch, nb, heads, hd = means.shape
    assert nb >= MOBA_TOPK
    return pl.pallas_call(
        functools.partial(_sample_gate_kernel, n_tok=n_tok, n_sel=MOBA_TOPK),
        grid=(batch,),
        in_specs=[pl.BlockSpec((1, n_tok, heads, hd), lambda b: (b, 0, 0, 0)),
                  pl.BlockSpec((1, nb, heads, hd), lambda b: (b, 0, 0, 0))],
        out_specs=pl.BlockSpec((1, n_tok * MOBA_TOPK, heads, 1), lambda b: (b, 0, 0, 0)),
        out_shape=jax.ShapeDtypeStruct((batch, n_tok * MOBA_TOPK, heads, 1), jnp.int32),
        compiler_params=_cparams("arbitrary"),
        name="sample_gate",
    )(q4, means)


def _sample_attn_kernel(pp_ref, q_ref, kn_ref, vn_ref, ck_hbm, cv_hbm, o_ref, kbuf, vbuf, sem,
                        *, n_steps, heads, n_tok, n_pg, hd):
    step = pl.program_id(0)
    per_step = n_tok * n_pg

    def copies(st, slot):
        h = st % heads
        out = []
        for i in range(per_step):
            pg = pp_ref[st * per_step + i]
            out.append(pltpu.make_async_copy(ck_hbm.at[0, pg, :, h, :], kbuf.at[slot, i], sem.at[0, slot]))
            out.append(pltpu.make_async_copy(cv_hbm.at[0, pg, :, h, :], vbuf.at[slot, i], sem.at[1, slot]))
        return out

    slot = step % 2

    @pl.when(step == 0)
    def _():
        for cp in copies(step, 0):
            cp.start()

    @pl.when(step + 1 < n_steps)
    def _():
        for cp in copies(step + 1, 1 - slot):
            cp.start()

    for cp in copies(step, slot):
        cp.wait()

    r = SAMPLE_ROWS
    scale = hd ** -0.5
    kn = kn_ref[...]
    vn = vn_ref[...]
    row = lax.broadcasted_iota(jnp.int32, (r, 1), 0)
    o_ref[...] = jnp.zeros_like(o_ref)
    for t in range(n_tok):
        q_t = q_ref[t:t + 1, :]
        s_own = jnp.sum(kn * q_t, axis=-1, keepdims=True) * scale
        s_own = jnp.where(row <= t, s_own, NEG_BIG)
        s_pg = [jnp.sum(kbuf[slot, t * n_pg + j] * q_t, axis=-1, keepdims=True) * scale for j in range(n_pg)]
        m = jnp.max(s_own, axis=0, keepdims=True)
        for s in s_pg:
            m = jnp.maximum(m, jnp.max(s, axis=0, keepdims=True))
        p_own = jnp.exp(s_own - m)
        l = jnp.sum(p_own, axis=0, keepdims=True)
        acc = jnp.sum(p_own * vn, axis=0, keepdims=True)
        for j in range(n_pg):
            p = jnp.exp(s_pg[j] - m)
            l = l + jnp.sum(p, axis=0, keepdims=True)
            acc = acc + jnp.sum(p * vbuf[slot, t * n_pg + j], axis=0, keepdims=True)
        o_ref[t:t + 1, :] = acc / l


def sample_attention(q8, kn8, vn8, cache_k, cache_v, pp_flat, *, batch, heads, hd, n_tok, page):
    ppb = MOBA_BLOCK // page
    n_pg = MOBA_TOPK * ppb
    n_steps = batch * heads
    small = pl.BlockSpec((SAMPLE_ROWS, hd), lambda s, pp: (s // heads, s % heads))
    return pl.pallas_call(
        functools.partial(_sample_attn_kernel, n_steps=n_steps, heads=heads, n_tok=n_tok, n_pg=n_pg, hd=hd),
        grid_spec=pltpu.PrefetchScalarGridSpec(
            num_scalar_prefetch=1,
            grid=(n_steps,),
            in_specs=[small, small, small, pl.BlockSpec(memory_space=pl.ANY), pl.BlockSpec(memory_space=pl.ANY)],
            out_specs=small,
            scratch_shapes=[pltpu.VMEM((2, n_tok * n_pg, page, hd), F32),
                            pltpu.VMEM((2, n_tok * n_pg, page, hd), F32),
                            pltpu.SemaphoreType.DMA((2, 2))],
        ),
        out_shape=jax.ShapeDtypeStruct((batch * SAMPLE_ROWS, heads * hd), F32),
        compiler_params=_cparams("arbitrary"),
        name="sample_attn",
    )(pp_flat, q8, kn8, vn8, cache_k, cache_v)


def _norm_router_kernel(xp_ref, xr_ref, w_ref, wr_ref, h_ref, info_ref, *, n_experts, n_head_tiles):
    x = jnp.where(pl.program_id(0) < n_head_tiles, xp_ref[...], xr_ref[...])
    y = x * lax.rsqrt(jnp.mean(x * x, axis=-1, keepdims=True) + RMS_EPS) * w_ref[...]
    h_ref[...] = y
    logits = jnp.dot(y, wr_ref[...], precision=lax.Precision.HIGHEST, preferred_element_type=F32)
    r, n = logits.shape
    col = lax.broadcasted_iota(jnp.int32, (r, n), 1)
    lg = jnp.where(col < n_experts, logits, -jnp.inf)
    m1 = jnp.max(lg, axis=-1, keepdims=True)
    i1 = jnp.min(jnp.where(lg == m1, col, n), axis=-1, keepdims=True)
    lg2 = jnp.where(col == i1, -jnp.inf, lg)
    m2 = jnp.max(lg2, axis=-1, keepdims=True)
    i2 = jnp.min(jnp.where(lg2 == m2, col, n), axis=-1, keepdims=True)
    e2 = jnp.exp(m2 - m1)
    den = 1.0 + e2
    w1 = 1.0 / den
    w2 = e2 / den
    info = jnp.where(col == 0, i1.astype(F32),
                     jnp.where(col == 1, i2.astype(F32),
                               jnp.where(col == 2, w1, jnp.where(col == 3, w2, 0.0))))
    info_ref[...] = info


def _head_rest_specs(tm, d, n_head_tiles):
    def head(i, *_):
        return (jnp.minimum(i, n_head_tiles - 1), 0)

    def rest(i, *_):
        return (jnp.maximum(i - n_head_tiles, 0), 0)

    return pl.BlockSpec((tm, d), head), pl.BlockSpec((tm, d), rest)


def norm_router(xp, xr, w, w_router):
    d = xp.shape[1]
    m = xp.shape[0] + xr.shape[0]
    e = w_router.shape[1]
    tm = ROW_ALIGN
    assert xp.shape[0] % tm == 0 and xr.shape[0] % tm == 0
    npt = xp.shape[0] // tm
    wr = jnp.zeros((d, 128), F32).at[:, :e].set(w_router)
    head, rest = _head_rest_specs(tm, d, npt)
    return pl.pallas_call(
        functools.partial(_norm_router_kernel, n_experts=e, n_head_tiles=npt),
        grid=(m // tm,),
        in_specs=[head, rest, pl.BlockSpec((1, d), lambda i: (0, 0)), pl.BlockSpec((d, 128), lambda i: (0, 0))],
        out_specs=[pl.BlockSpec((tm, d), lambda i: (i, 0)), pl.BlockSpec((tm, 128), lambda i: (i, 0))],
        out_shape=[jax.ShapeDtypeStruct((m, d), F32), jax.ShapeDtypeStruct((m, 128), F32)],
        compiler_params=_cparams("arbitrary"),
        name="norm_router",
    )(xp, xr, w.reshape(1, d), wr)


GATHER_TILE = 256


def _start_rows(src_hbm, dst_ref, sem, row_of):
    def start(r, c):
        pltpu.make_async_copy(src_hbm.at[pl.ds(row_of(r), 1), :], dst_ref.at[pl.ds(r, 1), :], sem).start()
        return c

    lax.fori_loop(0, dst_ref.shape[0], start, 0, unroll=8)


def _wait_rows(dst_ref, sem):
    pltpu.make_async_copy(dst_ref, dst_ref, sem).wait()


def _gather_cast_kernel(src_ref, ntile_ref, h_hbm, o_ref, buf_ref, sem):
    i = pl.program_id(0)
    n = ntile_ref[0]

    def issue(tile, slot):
        base = tile * GATHER_TILE
        _start_rows(h_hbm, buf_ref.at[slot], sem.at[slot], lambda r: src_ref[base + r])

    @pl.when(jnp.logical_and(i == 0, n > 0))
    def _():
        issue(0, 0)

    @pl.when(i + 1 < n)
    def _():
        issue(i + 1, (i + 1) % 2)

    @pl.when(i < n)
    def _():
        _wait_rows(buf_ref.at[i % 2], sem.at[i % 2])
        o_ref[...] = buf_ref[i % 2].astype(o_ref.dtype)

    @pl.when(i >= n)
    def _():
        o_ref[...] = jnp.zeros_like(o_ref)


def gather_cast(h, src_rows, n_tiles_used):
    s_rows = src_rows.shape[0]
    d = h.shape[1]
    return pl.pallas_call(
        _gather_cast_kernel,
        grid_spec=pltpu.PrefetchScalarGridSpec(
            num_scalar_prefetch=2,
            grid=(s_rows // GATHER_TILE,),
            in_specs=[pl.BlockSpec(memory_space=pl.ANY)],
            out_specs=pl.BlockSpec((GATHER_TILE, d), lambda i, s, n: (i, 0)),
            scratch_shapes=[pltpu.VMEM((2, GATHER_TILE, d), F32), pltpu.SemaphoreType.DMA((2,))],
        ),
        out_shape=jax.ShapeDtypeStruct((s_rows, d), BF16),
        compiler_params=_cparams("arbitrary"),
        name="gather_cast",
    )(src_rows, n_tiles_used, h)


EXPERT_SUB = 128
EXPERT_MAIN = 8
SEGMENT_ROWS = 2304


def _experts_kernel(seg_e_ref, seg_rows_ref, nseg_ref, x_ref, wg_ref, wu_ref, wd_ref, o_ref,
                    wgb_ref, wub_ref, wdb_ref):
    s = pl.program_id(0)
    f = pl.program_id(1)

    @pl.when(f == 0)
    def _():
        o_ref[...] = jnp.zeros_like(o_ref)

    def tiles(starts, size):
        gu = []
        for r0 in starts:
            xs = x_ref[pl.ds(r0, size), :]
            gu.append((_dot(xs, wgb_ref[...]), _dot(xs, wub_ref[...])))
        for r0, (g, u) in zip(starts, gu):
            a = (_silu(g) * u).astype(BF16)
            o_ref[pl.ds(r0, size), :] += _dot(a, wdb_ref[...])

    @pl.when(s < nseg_ref[0])
    def _():
        wgb_ref[...] = wg_ref[...].astype(BF16)
        wub_ref[...] = wu_ref[...].astype(BF16)
        wdb_ref[...] = wd_ref[...].astype(BF16)
        sub = EXPERT_SUB
        n_sub = (seg_rows_ref[s] + sub - 1) // sub
        n_main = n_sub // EXPERT_MAIN

        def body(i, c):
            r0 = pl.multiple_of(i * (EXPERT_MAIN * sub), EXPERT_MAIN * sub)
            half = EXPERT_MAIN // 2 * sub
            tiles([r0, r0 + half], half)
            return c

        lax.fori_loop(0, n_main, body, 0)
        done = n_main * EXPERT_MAIN
        size = EXPERT_MAIN // 2
        while size >= 1:
            take = ((n_sub - done) // size) > 0

            @pl.when(take)
            def _(done=done, size=size):
                tiles([pl.multiple_of(done * sub, sub)], size * sub)

            done = done + jnp.where(take, size, 0)
            size //= 2


def experts(x_sorted, wg, wu, wd, layer, seg_e, seg_rows, nseg, *, seg_cap):
    s_rows, d = x_sorted.shape
    s_max = s_rows // seg_cap
    f_dim = wg.shape[3]
    tf = _pick(f_dim, (256, 128))
    nf = f_dim // tf

    def seg_blk(s, n):
        return jnp.minimum(s, n[0] - 1)

    def fblk(s, f, n):
        return jnp.where(s < n[0], f, nf - 1)

    one = pl.Buffered(1)
    in_specs = [
        pl.BlockSpec((seg_cap, d), lambda s, f, e, r, n: (seg_blk(s, n), 0), pipeline_mode=one),
        pl.BlockSpec((None, None, d, tf), lambda s, f, e, r, n: (layer, e[s], 0, fblk(s, f, n))),
        pl.BlockSpec((None, None, d, tf), lambda s, f, e, r, n: (layer, e[s], 0, fblk(s, f, n))),
        pl.BlockSpec((None, None, tf, d), lambda s, f, e, r, n: (layer, e[s], fblk(s, f, n), 0)),
    ]
    return pl.pallas_call(
        _experts_kernel,
        grid_spec=pltpu.PrefetchScalarGridSpec(
            num_scalar_prefetch=3,
            grid=(s_max, nf),
            in_specs=in_specs,
            out_specs=pl.BlockSpec((seg_cap, d), lambda s, f, e, r, n: (s, 0), pipeline_mode=one),
            scratch_shapes=[pltpu.VMEM((d, tf), BF16), pltpu.VMEM((d, tf), BF16),
                            pltpu.VMEM((tf, d), BF16)],
        ),
        out_shape=jax.ShapeDtypeStruct((s_rows, d), F32),
        compiler_params=_cparams("arbitrary", "arbitrary"),
        name="experts",
    )(seg_e, seg_rows, nseg, x_sorted, wg, wu, wd)


def _combine_norm_kernel(s1_ref, s2_ref, ys_hbm, xp_ref, xr_ref, info_ref, w_ref, op_ref, or_ref, a_ref, b_ref,
                         sem, *, n_steps, n_head_tiles):
    i = pl.program_id(0)
    tm = xp_ref.shape[0]

    def issue(tile, slot):
        base = tile * tm
        _start_rows(ys_hbm, a_ref.at[slot], sem.at[0, slot], lambda r: s1_ref[base + r])
        _start_rows(ys_hbm, b_ref.at[slot], sem.at[1, slot], lambda r: s2_ref[base + r])

    @pl.when(i == 0)
    def _():
        issue(0, 0)

    @pl.when(i + 1 < n_steps)
    def _():
        issue(i + 1, (i + 1) % 2)

    slot = i % 2
    _wait_rows(a_ref.at[slot], sem.at[0, slot])
    _wait_rows(b_ref.at[slot], sem.at[1, slot])
    info = info_ref[...]
    y = info[:, 2:3] * a_ref[slot] + info[:, 3:4] * b_ref[slot]
    x = jnp.where(i < n_head_tiles, xp_ref[...], xr_ref[...]) + y
    xn = x * lax.rsqrt(jnp.mean(x * x, axis=-1, keepdims=True) + RMS_EPS)
    out = xn * w_ref[...]

    @pl.when(i < n_head_tiles)
    def _():
        op_ref[...] = out

    @pl.when(i >= n_head_tiles)
    def _():
        or_ref[...] = out


def combine_norm(xp, xr, y_sorted, slot1, slot2, info, w):
    d = xp.shape[1]
    m = xp.shape[0] + xr.shape[0]
    tm = ROW_ALIGN
    assert xp.shape[0] % tm == 0 and xr.shape[0] % tm == 0
    npt = xp.shape[0] // tm
    head, rest = _head_rest_specs(tm, d, npt)
    return pl.pallas_call(
        functools.partial(_combine_norm_kernel, n_steps=m // tm, n_head_tiles=npt),
        grid_spec=pltpu.PrefetchScalarGridSpec(
            num_scalar_prefetch=2,
            grid=(m // tm,),
            in_specs=[pl.BlockSpec(memory_space=pl.ANY), head, rest,
                      pl.BlockSpec((tm, 128), lambda i, a, b: (i, 0)),
                      pl.BlockSpec((1, d), lambda i, a, b: (0, 0))],
            out_specs=[head, rest],
            scratch_shapes=[pltpu.VMEM((2, tm, d), F32), pltpu.VMEM((2, tm, d), F32),
                            pltpu.SemaphoreType.DMA((2, 2))],
        ),
        out_shape=[jax.ShapeDtypeStruct(xp.shape, F32), jax.ShapeDtypeStruct(xr.shape, F32)],
        compiler_params=_cparams("arbitrary"),
        name="combine_norm",
    )(slot1, slot2, y_sorted, xp, xr, info, w.reshape(1, d))


def _route_tables(info, n_real, n_experts, seg_cap):
    m = info.shape[0]
    e = info[:, :MOE_TOPK].astype(jnp.int32)
    valid = (jnp.arange(m) < n_real)[:, None]
    onehot = ((e[:, :, None] == jnp.arange(n_experts)[None, None, :]) & valid[:, :, None])
    onehot = onehot.reshape(m * MOE_TOPK, n_experts).astype(jnp.int32)
    rank = jnp.sum((jnp.cumsum(onehot, axis=0) - onehot) * onehot, axis=1)
    counts = jnp.sum(onehot, axis=0)
    nseg_e = (counts + seg_cap - 1) // seg_cap
    seg_start = jnp.cumsum(nseg_e) - nseg_e
    nseg = jnp.sum(nseg_e)
    s_max = (m * MOE_TOPK) // seg_cap + n_experts
    flat_e = e.reshape(-1)
    flat_valid = jnp.broadcast_to(valid, (m, MOE_TOPK)).reshape(-1)
    slot = (seg_start[flat_e] + rank // seg_cap) * seg_cap + rank % seg_cap
    slot = jnp.where(flat_valid, slot, 0)
    sid = jnp.arange(s_max)
    owner = jnp.argmax((sid[:, None] >= seg_start[None, :]) & (sid[:, None] < (seg_start + nseg_e)[None, :]), axis=1)
    seg_e = jnp.where(sid < nseg, owner, owner[jnp.maximum(nseg - 1, 0)]).astype(jnp.int32)
    within = sid - seg_start[seg_e]
    seg_rows = jnp.clip(counts[seg_e] - within * seg_cap, 0, seg_cap)
    seg_rows = jnp.where(sid < nseg, seg_rows, 0).astype(jnp.int32)
    tok = jnp.repeat(jnp.arange(m, dtype=jnp.int32), MOE_TOPK)
    src = jnp.full((s_max * seg_cap,), m - 1, jnp.int32)
    src = src.at[jnp.where(flat_valid, slot, s_max * seg_cap)].set(tok, mode="drop")
    n_tiles_used = (nseg * (seg_cap // GATHER_TILE)).astype(jnp.int32).reshape(1)
    slots = slot.reshape(m, MOE_TOPK).astype(jnp.int32)
    return src, n_tiles_used, seg_e, seg_rows, nseg.astype(jnp.int32).reshape(1), slots[:, 0], slots[:, 1]


def kernel(x_prompt, x_sample, state_ret, cache_k, cache_v, page_table, norm_w, final_norm_w,
           ret_w_in, ret_w_out, moba_w_qkv, moba_w_out, ffn_w_gate, ffn_w_up, ffn_w_down,
           moe_w_router, moe_w_gate, moe_w_up, moe_w_down):
    bp, sp, d = x_prompt.shape
    bs, ts, _ = x_sample.shape
    _, _, r_heads, dk, dv = state_ret.shape
    _, n_pool, page, m_heads, hd = cache_k.shape
    n_pages = page_table.shape[1]
    past_len = n_pages * page
    n_experts = moe_w_router.shape[2]
    assert sp % RET_CHUNK == 0 and ts < RET_CHUNK and ts <= SAMPLE_ROWS
    assert MOBA_BLOCK % page == 0 and past_len % MOBA_BLOCK == 0 and sp % MOBA_BLOCK == 0

    n_p = bp * sp
    n_s = bs * ts
    n_real = n_p + n_s
    xp = x_prompt.reshape(n_p, d)
    xs = x_sample.reshape(n_s, d)

    hp = rmsnorm(xp, norm_w[0, 0], BF16)
    hs = rmsnorm(xs, norm_w[0, 0], F32)
    n_qk = 2 * r_heads * dk
    qk = matmul(hp, ret_w_in, 0, n_off=0, n=n_qk, name="ret_qk_proj")
    vg = matmul(hp, ret_w_in, 0, n_off=n_qk, n=2 * r_heads * dv, out_dtype=BF16, name="ret_vg_proj")
    pos_p = jnp.arange(sp, dtype=jnp.int32)
    zero_state = jnp.zeros((bp, r_heads, dk, dv), F32)
    y_p, st_p = retention(qk, vg, zero_state, pos_p, batch=bp, n_chunks=sp // RET_CHUNK,
                          chunk=RET_CHUNK, valid=RET_CHUNK, heads=r_heads, dk=dk, dv=dv)
    qkvg_s = matmul_precise(hs, ret_w_in, 0, name="ret_in_proj_s")

    def pad_chunk(a2):
        a3 = jnp.pad(a2.reshape(bs, ts, -1), ((0, 0), (0, RET_CHUNK - ts), (0, 0)))
        return a3.reshape(bs * RET_CHUNK, -1)

    pos_s = past_len + jnp.arange(RET_CHUNK, dtype=jnp.int32)
    y_s, st_s = retention(pad_chunk(qkvg_s[:, :n_qk]), pad_chunk(qkvg_s[:, n_qk:]), state_ret[0], pos_s,
                          batch=bs, n_chunks=1, chunk=RET_CHUNK, valid=ts, heads=r_heads, dk=dk, dv=dv,
                          precise=True)
    y_s = y_s.reshape(bs, RET_CHUNK, -1)[:, :ts].reshape(n_s, -1)
    xp = matmul(y_p, ret_w_out, 0, res=xp, name="ret_out_proj")
    xs = matmul_precise(y_s, ret_w_out, 0, res=xs, name="ret_out_proj_s")

    hp = rmsnorm(xp, norm_w[0, 1], BF16)
    hs = rmsnorm(xs, norm_w[0, 1], F32)
    xp = matmul(gate_up(hp, ffn_w_gate, ffn_w_up, 0), ffn_w_down, 0, res=xp, name="ffn_down")
    xs = matmul_precise(gate_up_precise(hs, ffn_w_gate, ffn_w_up, 0), ffn_w_down, 0, res=xs, name="ffn_down_s")

    hp = rmsnorm(xp, norm_w[1, 0], BF16)
    hs = rmsnorm(xs, norm_w[1, 0], F32)
    q = matmul(hp, moba_w_qkv, 0, n_off=0, n=d, name="moba_q_proj")
    k = matmul(hp, moba_w_qkv, 0, n_off=d, n=d, name="moba_k_proj")
    v = matmul(hp, moba_w_qkv, 0, n_off=2 * d, n=d, name="moba_v_proj")
    o_p = moba_prompt(q, k, v, batch=bp, seq=sp, heads=m_heads, hd=hd)
    qkv_s = matmul_precise(hs, moba_w_qkv, 0, name="moba_qkv_proj_s")
    q_s, k_s, v_s = qkv_s[:, :d], qkv_s[:, d:2 * d], qkv_s[:, 2 * d:]

    def pad8(a2):
        return jnp.pad(a2.reshape(bs, ts, d), ((0, 0), (0, SAMPLE_ROWS - ts), (0, 0))).reshape(bs * SAMPLE_ROWS, d)

    means = page_block_means(cache_k, page_table.reshape(-1), batch=bs, n_pages=n_pages)
    sel = sample_gate(q_s.reshape(bs, ts, m_heads, hd), means, n_tok=ts)
    sel = sel.reshape(bs, ts, MOBA_TOPK, m_heads).transpose(0, 3, 1, 2)
    ppb = MOBA_BLOCK // page
    lpage = sel[..., None] * ppb + jnp.arange(ppb, dtype=jnp.int32)
    pp = page_table[jnp.arange(bs)[:, None, None, None], lpage.reshape(bs, m_heads, ts, MOBA_TOPK * ppb)]
    o_s8 = sample_attention(pad8(q_s), pad8(k_s), pad8(v_s), cache_k, cache_v, pp.reshape(-1).astype(jnp.int32),
                            batch=bs, heads=m_heads, hd=hd, n_tok=ts, page=page)
    o_s = o_s8.reshape(bs, SAMPLE_ROWS, d)[:, :ts].reshape(n_s, d)
    xp = matmul(o_p, moba_w_out, 0, res=xp, name="moba_out_proj")
    xs = matmul_precise(o_s, moba_w_out, 0, res=xs, name="moba_out_proj_s")

    m = -(-(n_real + 1) // ROW_ALIGN) * ROW_ALIGN
    xr = jnp.concatenate([xs, jnp.zeros((m - n_real, d), F32)], axis=0)
    hf, info = norm_router(xp, xr, norm_w[1, 1], moe_w_router[0])
    seg_cap = SEGMENT_ROWS if m * MOE_TOPK >= 4 * SEGMENT_ROWS else 2 * GATHER_TILE
    src, n_tiles_used, seg_e, seg_rows, nseg, slot1, slot2 = _route_tables(info, n_real, n_experts, seg_cap)
    x_sorted = gather_cast(hf, src, n_tiles_used)
    y_sorted = experts(x_sorted, moe_w_gate, moe_w_up, moe_w_down, 0, seg_e, seg_rows, nseg, seg_cap=seg_cap)
    y_p, y_r = combine_norm(xp, xr, y_sorted, slot1, slot2, info, final_norm_w)

    y_prompt = y_p.reshape(bp, sp, d)
    y_sample = y_r[:n_s].reshape(bs, ts, d)
    kv_shape_p = (1, bp, sp, m_heads, hd)
    kv_shape_s = (1, bs, ts, m_heads, hd)
    return (y_prompt, y_sample, st_p[None], st_s[None],
            k.reshape(kv_shape_p), v.reshape(kv_shape_p), k_s.reshape(kv_shape_s), v_s.reshape(kv_shape_s))
```
